```python
import jax, jax.numpy as jnp
from jax import lax
import numpy as np


D_MODEL = 1024
BATCH = 8
SEQ = 4096
DEPTH = 2
DEC_BATCH = 32
DEC_SEQ = 1
PAST_LEN = 16384
PAGE_SIZE = 128

HG_WIDTH = D_MODEL // 4
HG_DK = 64
HG_DV = 64
HG_HEADS = HG_WIDTH // HG_DV
FOX_WIDTH = D_MODEL // 2
FOX_HEAD_DIM = 64
FOX_HEADS = FOX_WIDTH // FOX_HEAD_DIM
GLA_WIDTH = D_MODEL // 4
GLA_DV = 64
GLA_HEADS = GLA_WIDTH // GLA_DV
GLA_DK = GLA_DV // 2
GLA_GATE_RANK = 16
GLA_GATE_NORMALIZER = 16.0
CHUNK = 64
Q_BLOCK = 128
FF_DIM = ((8 * D_MODEL + 3 * 256 - 1) // (3 * 256)) * 256
RMS_EPS = 1e-6
FOX_F_BIAS_CENTER = 9.0
IN_SIZES = (HG_HEADS * HG_DK, HG_HEADS * HG_DK, HG_WIDTH, HG_WIDTH,
            FOX_WIDTH, FOX_WIDTH, FOX_WIDTH, FOX_HEADS,
            GLA_HEADS * GLA_DK, GLA_HEADS * GLA_DK, GLA_WIDTH, GLA_WIDTH, GLA_GATE_RANK)
IN_WIDTH = sum(IN_SIZES)
MIX_WIDTH = HG_WIDTH + FOX_WIDTH + GLA_WIDTH

kernel_name = 'hymba_hgrn2_fox_gla_step'


def _rmsnorm(x, w):
    xf = x.astype(jnp.float32)
    y = xf * lax.rsqrt(jnp.mean(xf * xf, axis=-1, keepdims=True) + RMS_EPS)
    return (y * w.astype(jnp.float32)).astype(x.dtype)


def _head_rmsnorm(o, w):
    y = o * lax.rsqrt(jnp.mean(o * o, axis=-1, keepdims=True) + RMS_EPS) * w.astype(jnp.float32)
    return y.reshape(o.shape[0], o.shape[1], -1)


def _project(x, lb_l, norm_w, w_in_l, fox_bf_l, gla_w2_l, gla_b_l):
    B, L, _ = x.shape
    h = _rmsnorm(x, norm_w)
    z = jnp.einsum('bld,de->ble', h, w_in_l).astype(jnp.float32)
    idx = np.cumsum(IN_SIZES)[:-1].tolist()
    hq, hf, hi, hg, fq, fk, fv, ff, gq, gk, gv, gg, glr = jnp.split(z, idx, axis=-1)
    lb = lb_l.astype(jnp.float32)
    f = lb + (1.0 - lb) * jax.nn.sigmoid(hf)
    hg_logf = jnp.log(f).reshape(B, L, HG_HEADS, HG_DK)
    hg_key = (1.0 - f).reshape(B, L, HG_HEADS, HG_DK)
    hg_q = hq.reshape(B, L, HG_HEADS, HG_DK)
    hg_v = hi.reshape(B, L, HG_HEADS, HG_DV)
    fox_q = fq.reshape(B, L, FOX_HEADS, FOX_HEAD_DIM) * (FOX_HEAD_DIM ** -0.5)
    fox_k = fk.reshape(B, L, FOX_HEADS, FOX_HEAD_DIM)
    fox_v = fv.reshape(B, L, FOX_HEADS, FOX_HEAD_DIM)
    fox_logf = jax.nn.log_sigmoid(ff + fox_bf_l.astype(jnp.float32))
    g_logit = jnp.einsum('blr,re->ble', glr, gla_w2_l.astype(jnp.float32)) + gla_b_l.astype(jnp.float32)
    gla_logg = (jax.nn.log_sigmoid(g_logit) / GLA_GATE_NORMALIZER).reshape(B, L, GLA_HEADS, GLA_DK)
    gla_q = gq.reshape(B, L, GLA_HEADS, GLA_DK) * (GLA_DK ** -0.5)
    gla_k = gk.reshape(B, L, GLA_HEADS, GLA_DK)
    gla_v = gv.reshape(B, L, GLA_HEADS, GLA_DV)
    return (hg_q, hg_key, hg_v, hg_logf, hg, fox_q, fox_k, fox_v, fox_logf,
            gla_q, gla_k, gla_v, gla_logg, gg)


def _chunked_gated_linear(q, k, v, log_g, s0):
    B, L, H, dk = q.shape
    dv = v.shape[-1]
    c = min(CHUNK, L)
    n = -(-L // c)
    pad = n * c - L

    def prep(t):
        t = jnp.pad(t, ((0, 0), (0, pad), (0, 0), (0, 0)))
        return t.reshape(B, n, c, H, t.shape[-1]).swapaxes(0, 1)

    xs = (prep(q), prep(k), prep(v), prep(log_g))
    mask = jnp.tril(jnp.ones((c, c), dtype=bool))[None, :, :, None, None]

    def step(S, blk):
        qc, kc, vc, gc = blk
        b = jnp.cumsum(gc, axis=1)
        o_inter = jnp.einsum('bthk,bhkv->bthv', qc * jnp.exp(b), S)
        decay = jnp.exp(jnp.where(mask, b[:, :, None] - b[:, None, :], -jnp.inf))
        scores = jnp.einsum('bthk,bshk,btshk->bhts', qc, kc, decay)
        o_intra = jnp.einsum('bhts,bshv->bthv', scores, vc)
        b_end = b[:, -1]
        k_dec = kc * jnp.exp(b_end[:, None] - b)
        S = jnp.exp(b_end)[..., None] * S + jnp.einsum('bshk,bshv->bhkv', k_dec, vc)
        return S, o_inter + o_intra

    S, o = lax.scan(step, s0.astype(jnp.float32), xs)
    o = o.swapaxes(0, 1).reshape(B, n * c, H, dv)[:, :L]
    return o, S


def _fox_prompt(q, k, v, logf):
    B, L, H, D = q.shape
    nb = L // Q_BLOCK
    c = lax.cumsum(logf, axis=1).transpose(0, 2, 1)
    qb = q.reshape(B, nb, Q_BLOCK, H, D).swapaxes(0, 1)
    cb = c.reshape(B, H, nb, Q_BLOCK).transpose(2, 0, 1, 3)
    kpos = jnp.arange(L)

    def block(args):
        i, qi, ci = args
        s = jnp.einsum('bqhd,bkhd->bhqk', qi, k) + (ci[..., :, None] - c[:, :, None, :])
        qpos = i * Q_BLOCK + jnp.arange(Q_BLOCK)
        s = jnp.where(kpos[None, :] <= qpos[:, None], s, -jnp.inf)
        p = jax.nn.softmax(s, axis=-1)
        return jnp.einsum('bhqk,bkhd->bqhd', p, v)

    o = lax.map(block, (jnp.arange(nb), qb, cb))
    return o.swapaxes(0, 1).reshape(B, L, H, D)


def _fox_sample(q, k, v, logf, cache_k, cache_v, cache_logf, page_table):
    DB, L, H, D = q.shape
    P = page_table.shape[1] * cache_k.shape[1]
    k_past = cache_k[page_table].reshape(DB, P, H, D).astype(jnp.float32)
    v_past = cache_v[page_table].reshape(DB, P, H, D).astype(jnp.float32)
    lf_past = cache_logf[page_table].reshape(DB, P, H).astype(jnp.float32)
    lf_all = jnp.concatenate([lf_past, logf], axis=1)
    rc = lax.cumsum(lf_all, axis=1, reverse=True)
    suffix = jnp.concatenate([rc[:, 1:], jnp.zeros_like(rc[:, :1])], axis=1)
    suffix = suffix.transpose(0, 2, 1)
    suf_t = suffix[:, :, P:]
    s_past = jnp.einsum('bqhd,bkhd->bhqk', q, k_past) + (suffix[:, :, None, :P] - suf_t[..., :, None])
    s_new = jnp.einsum('bqhd,bkhd->bhqk', q, k) + (suffix[:, :, None, P:] - suf_t[..., :, None])
    causal = jnp.tril(jnp.ones((L, L), dtype=bool))
    s_new = jnp.where(causal, s_new, -jnp.inf)
    p = jax.nn.softmax(jnp.concatenate([s_past, s_new], axis=-1), axis=-1)
    return (jnp.einsum('bhqk,bkhd->bqhd', p[..., :P], v_past)
            + jnp.einsum('bhqk,bkhd->bqhd', p[..., P:], v))


def _merge_and_ffn(x, o_hg, g_hg, o_fox, o_gla, g_gla,
                   hg_norm_l, gla_norm_l, w_out_l, ffn_norm_l, w_gu_l, w_down_l):
    B, L, _ = x.shape
    hg = _head_rmsnorm(o_hg, hg_norm_l) * jax.nn.silu(g_hg)
    gla = _head_rmsnorm(o_gla, gla_norm_l) * jax.nn.silu(g_gla)
    fox = o_fox.reshape(B, L, -1)
    mix = jnp.concatenate([hg, fox, gla], axis=-1).astype(x.dtype)
    x = x + jnp.einsum('ble,ed->bld', mix, w_out_l)
    h = _rmsnorm(x, ffn_norm_l)
    gate, up = jnp.split(jnp.einsum('bld,df->blf', h, w_gu_l), 2, axis=-1)
    return x + jnp.einsum('blf,fd->bld', jax.nn.silu(gate) * up, w_down_l)


def setup_inputs(seed: int = 0) -> dict:
    key = jax.random.key(seed)
    ks = jax.random.split(key, 24)
    f32 = jnp.float32
    n_pages = PAST_LEN // PAGE_SIZE
    n_used = DEC_BATCH * n_pages
    n_pool = n_used + max(n_used // 4, 1)
    nrm = lambda k, shape: jax.random.normal(k, shape, f32)
    return {
        'x_prompt': nrm(ks[0], (BATCH, SEQ, D_MODEL)),
        'x_sample': nrm(ks[1], (DEC_BATCH, DEC_SEQ, D_MODEL)),
        'cache_fox_k': nrm(ks[2], (DEPTH, n_pool, PAGE_SIZE, FOX_HEADS, FOX_HEAD_DIM)),
        'cache_fox_v': nrm(ks[3], (DEPTH, n_pool, PAGE_SIZE, FOX_HEADS, FOX_HEAD_DIM)),
        'cache_fox_logf': jax.nn.log_sigmoid(FOX_F_BIAS_CENTER + nrm(ks[4], (DEPTH, n_pool, PAGE_SIZE, FOX_HEADS))),
        'state_hgrn': 0.5 * nrm(ks[5], (DEPTH, DEC_BATCH, HG_HEADS, HG_DK, HG_DV)),
        'state_gla': 0.5 * nrm(ks[6], (DEPTH, DEC_BATCH, GLA_HEADS, GLA_DK, GLA_DV)),
        'page_table': jax.random.permutation(ks[7], n_pool)[:n_used].reshape(DEC_BATCH, n_pages).astype(jnp.int32),
        'attn_norm': 1.0 + 0.01 * nrm(ks[8], (DEPTH, D_MODEL)),
        'w_in': nrm(ks[9], (DEPTH, D_MODEL, IN_WIDTH)) * D_MODEL ** -0.5,
        'fox_f_bias': FOX_F_BIAS_CENTER + 0.5 * nrm(ks[10], (DEPTH, FOX_HEADS)),
        'gla_gate_w2': nrm(ks[11], (DEPTH, GLA_GATE_RANK, GLA_HEADS * GLA_DK)) * GLA_GATE_RANK ** -0.5,
        'gla_gate_b': 0.1 * nrm(ks[12], (DEPTH, GLA_HEADS * GLA_DK)),
        'hgrn_lb_logits': nrm(ks[13], (DEPTH, HG_HEADS * HG_DK)),
        'hgrn_out_norm': 1.0 + 0.01 * nrm(ks[14], (DEPTH, HG_DV)),
        'gla_out_norm': 1.0 + 0.01 * nrm(ks[15], (DEPTH, GLA_DV)),
        'w_out': nrm(ks[16], (DEPTH, MIX_WIDTH, D_MODEL)) * MIX_WIDTH ** -0.5,
        'ffn_norm': 1.0 + 0.01 * nrm(ks[17], (DEPTH, D_MODEL)),
        'w_gate_up': nrm(ks[18], (DEPTH, D_MODEL, 2 * FF_DIM)) * D_MODEL ** -0.5,
        'w_down': nrm(ks[19], (DEPTH, FF_DIM, D_MODEL)) * FF_DIM ** -0.5,
        'final_norm': 1.0 + 0.01 * nrm(ks[20], (D_MODEL,)),
    }


def reference(x_prompt, x_sample, cache_fox_k, cache_fox_v, cache_fox_logf, state_hgrn, state_gla,
              page_table, attn_norm, w_in, fox_f_bias, gla_gate_w2, gla_gate_b, hgrn_lb_logits,
              hgrn_out_norm, gla_out_norm, w_out, ffn_norm, w_gate_up, w_down, final_norm):
    B = x_prompt.shape[0]
    lb = jnp.cumsum(jax.nn.softmax(hgrn_lb_logits.astype(jnp.float32), axis=0), axis=0)
    lb = lb - lb[0]
    xp, xs = x_prompt, x_sample
    kp, vp, lfp, hsp, gsp = [], [], [], [], []
    ksm, vsm, lfs, hss, gss = [], [], [], [], []
    for l in range(DEPTH):
        pw = (lb[l], attn_norm[l], w_in[l], fox_f_bias[l], gla_gate_w2[l], gla_gate_b[l])
        mw = (hgrn_out_norm[l], gla_out_norm[l], w_out[l], ffn_norm[l], w_gate_up[l], w_down[l])
        (hq, hk, hv, hlf, hg, fq, fk, fv, flf, gq, gk, gv, glg, gg) = _project(xp, *pw)
        o_hg, s_hg = _chunked_gated_linear(hq, hk, hv, hlf, jnp.zeros((B, HG_HEADS, HG_DK, HG_DV), jnp.float32))
        o_fox = _fox_prompt(fq, fk, fv, flf)
        o_gla, s_gla = _chunked_gated_linear(gq, gk, gv, glg, jnp.zeros((B, GLA_HEADS, GLA_DK, GLA_DV), jnp.float32))
        xp = _merge_and_ffn(xp, o_hg, hg, o_fox, o_gla, gg, *mw)
        kp.append(fk); vp.append(fv); lfp.append(flf); hsp.append(s_hg); gsp.append(s_gla)
        (hq, hk, hv, hlf, hg, fq, fk, fv, flf, gq, gk, gv, glg, gg) = _project(xs, *pw)
        o_hg, s_hg = _chunked_gated_linear(hq, hk, hv, hlf, state_hgrn[l])
        o_fox = _fox_sample(fq, fk, fv, flf, cache_fox_k[l], cache_fox_v[l], cache_fox_logf[l], page_table)
        o_gla, s_gla = _chunked_gated_linear(gq, gk, gv, glg, state_gla[l])
        xs = _merge_and_ffn(xs, o_hg, hg, o_fox, o_gla, gg, *mw)
        ksm.append(fk); vsm.append(fv); lfs.append(flf); hss.append(s_hg); gss.append(s_gla)
    y_prompt = _rmsnorm(xp, final_norm)
    y_sample = _rmsnorm(xs, final_norm)
    return (y_prompt, y_sample,
            jnp.stack(kp), jnp.stack(vp), jnp.stack(lfp), jnp.stack(hsp), jnp.stack(gsp),
            jnp.stack(ksm), jnp.stack(vsm), jnp.stack(lfs), jnp.stack(hss), jnp.stack(gss))
```

```python
import functools

import numpy as np
import jax
import jax.numpy as jnp
from jax import lax
from jax.experimental import pallas as pl
from jax.experimental.pallas import tpu as pltpu

F32 = jnp.float32
BF16 = jnp.bfloat16

HG_DK = 64
HG_DV = 64
FOX_HEAD_DIM = 64
GLA_DK = 32
GLA_DV = 64
GLA_GATE_RANK = 16
GLA_GATE_NORMALIZER = 16.0
RMS_EPS = 1e-6

LANES = 128
CHUNK = 64
SUB = 16
NSUB = CHUNK // SUB
NEG_BIG = -1e30
VMEM_LIMIT = 48 * 1024 * 1024


def _cparams(sem):
    return pltpu.CompilerParams(dimension_semantics=sem, vmem_limit_bytes=VMEM_LIMIT)


def _shift(n):
    s = int(np.log2(n))
    assert (1 << s) == n
    return s


def _iota(shape, dim):
    return lax.broadcasted_iota(jnp.int32, shape, dim)


def _split3(x):
    x1 = x.astype(BF16)
    r1 = x - x1.astype(F32)
    x2 = r1.astype(BF16)
    r2 = r1 - x2.astype(F32)
    return x1, x2, r2.astype(BF16)


def _log_sigmoid(x):
    return jnp.minimum(x, 0.0) - jnp.log1p(jnp.exp(-jnp.abs(x)))


def _sigmoid(x):
    return 1.0 / (1.0 + jnp.exp(-x))


def _dot(a, b):
    return jnp.dot(a, b, preferred_element_type=F32)


def _dot_nt(a, b):
    return lax.dot_general(a, b, (((1,), (1,)), ((), ())), preferred_element_type=F32)


def _dot_tn(a, b):
    return lax.dot_general(a, b, (((0,), (0,)), ((), ())), preferred_element_type=F32)


def _rms(x, w):
    ms = jnp.mean(x * x, axis=-1, keepdims=True)
    return x * lax.rsqrt(ms + RMS_EPS) * w


W_PAD = 3456
MISC = 3328


def _proj_kernel(layer, with_cumsum, *refs):
    (x_ref, norm_ref, w_ref, fbias_ref, lbl_ref, w2_ref, gb_ref) = refs[:7]
    k = 7
    if with_cumsum:
        tri_ref = refs[k]
        k += 1
    (hq_ref, hk_ref, hlf_ref, hv_ref, hgate_ref, fq_ref, fk_ref, fv_ref, flf_ref,
     gq_ref, gk_ref, gv_ref, glg_ref, ggate_ref) = refs[k:k + 14]
    k += 14
    if with_cumsum:
        ct_ref, carry_ref = refs[k], refs[k + 1]

    hb = _rms(x_ref[0], norm_ref[...]).astype(BF16)

    def seg(a, b):
        return _dot(hb, w_ref[:, a:b])

    lbl = lbl_ref[...]
    e = jnp.exp(lbl - jnp.max(lbl, axis=0, keepdims=True))
    sm = e / jnp.sum(e, axis=0, keepdims=True)
    lb = jnp.zeros((1, lbl.shape[1]), F32)
    for i in range(1, layer + 1):
        lb = lb + sm[i:i + 1, :]
    hq_ref[0] = seg(0, 256)
    f = lb + (1.0 - lb) * _sigmoid(seg(256, 512))
    hlf_ref[0] = jnp.log(f)
    hk_ref[0] = 1.0 - f
    hv_ref[0] = seg(512, 768).astype(hv_ref.dtype)
    hgate_ref[0] = seg(768, 1024)
    fq_ref[0] = (seg(1024, 1536) * (FOX_HEAD_DIM ** -0.5)).astype(fq_ref.dtype)
    fk_ref[0] = seg(1536, 2048)
    fv_ref[0] = seg(2048, 2560)
    gq_ref[0] = seg(2560, 2688) * (GLA_DK ** -0.5)
    gk_ref[0] = seg(2688, 2816)
    gv_ref[0] = seg(2816, 3072).astype(gv_ref.dtype)
    ggate_ref[0] = seg(3072, 3328)
    misc = seg(MISC, W_PAD)
    lf_all = _log_sigmoid(misc + fbias_ref[...])
    n_fh = flf_ref.shape[-1]
    flf_ref[0] = lf_all[:, 0:n_fh]
    g_logit = _dot(misc.astype(BF16), w2_ref[...]) + gb_ref[...]
    glg_ref[0] = _log_sigmoid(g_logit) / GLA_GATE_NORMALIZER

    if with_cumsum:
        t = pl.program_id(1)

        @pl.when(t == 0)
        def _():
            carry_ref[...] = jnp.zeros_like(carry_ref)

        lft = lf_all.T[0:8, :]
        x1, x2, x3 = _split3(lft)
        cat = jnp.concatenate([x1.astype(F32), x2.astype(F32), x3.astype(F32),
                               jnp.zeros_like(lft)], axis=0).astype(BF16)
        c = _dot(cat, tri_ref[...])
        ct = c[0:8] + c[8:16] + c[16:24] + carry_ref[:, 0:1]
        ct_ref[0] = ct
        tm = ct.shape[1]
        carry_ref[...] = jnp.broadcast_to(ct[:, tm - 1:tm], carry_ref.shape)


def _project(x3, layer, with_cumsum, tm, v_dtype, pw):
    B, L, D = x3.shape
    n_fh = pw["n_fox_heads"]
    grid = (B, L // tm)
    row = lambda w: pl.BlockSpec((1, tm, w), lambda b, t: (b, t, 0))
    const = lambda shp: pl.BlockSpec(shp, lambda b, t: (0,) * len(shp))
    widths = [(256, F32), (256, F32), (256, F32), (256, v_dtype), (256, F32),
              (512, BF16), (512, F32), (512, F32), (n_fh, F32),
              (128, F32), (128, F32), (256, v_dtype), (128, F32), (256, F32)]
    in_arrays = [x3, pw["attn_norm"][layer], pw["w_in"][layer], pw["fbias"][layer],
                 pw["lb_logits"], pw["w2pad"][layer], pw["gla_b"][layer]]
    in_specs = [row(D), const((1, D)), const((D, W_PAD)), const((1, LANES)),
                const(pw["lb_logits"].shape), const((LANES, LANES)), const((1, LANES))]
    out_shape = [jax.ShapeDtypeStruct((B, L, w), dt) for w, dt in widths]
    out_specs = [row(w) for w, _ in widths]
    scratch = []
    if with_cumsum:
        tri = jnp.asarray(np.triu(np.ones((tm, tm), np.float32)), BF16)
        in_arrays.append(tri)
        in_specs.append(const((tm, tm)))
        out_shape.append(jax.ShapeDtypeStruct((B, 8, L), F32))
        out_specs.append(pl.BlockSpec((1, 8, tm), lambda b, t: (b, 0, t)))
        scratch.append(pltpu.VMEM((8, LANES), F32))
    return pl.pallas_call(
        functools.partial(_proj_kernel, layer, with_cumsum),
        grid=grid, in_specs=in_specs, out_specs=out_specs, out_shape=out_shape,
        scratch_shapes=scratch,
        compiler_params=_cparams(("parallel", "arbitrary")),
        name="proj_prompt" if with_cumsum else "proj_sample",
    )(*in_arrays)


def _gla_chunk_kernel(dk, dv, n_chunks, q_ref, k_ref, g_ref, v_ref, tt_ref, o_ref, st_ref, s_scr):
    hdk = q_ref.shape[-1]
    hdv = v_ref.shape[-1]
    nh = hdk // dk
    C = CHUNK

    @pl.when(pl.program_id(1) == 0)
    def _():
        s_scr[...] = jnp.zeros_like(s_scr)

    row_k = _iota((C, hdk), 0)
    head_k = _iota((C, hdk), 1) >> _shift(dk)
    head_v = _iota((C, hdv), 1) >> _shift(dv)
    a_row = _iota((C, nh * C), 0)
    a_col = _iota((C, nh * C), 1) & (C - 1)
    causal = a_col <= a_row
    bd_mask = (_iota((hdv, hdk), 0) >> _shift(dv)) == (_iota((hdv, hdk), 1) >> _shift(dk))

    def body(ci, carry):
        r0 = pl.multiple_of(ci * C, C)
        q = q_ref[0, pl.ds(r0, C), :]
        k = k_ref[0, pl.ds(r0, C), :]
        g = g_ref[0, pl.ds(r0, C), :]
        v = v_ref[0, pl.ds(r0, C), :].astype(BF16)
        g1, g2, g3 = _split3(g)
        gcat = jnp.concatenate([g1, g2, g3], axis=0)
        bw = _dot(tt_ref[...], gcat)
        b = bw[0:C]
        w = bw[C:2 * C]
        qt = q * jnp.exp(w)
        qb = (q * jnp.exp(b)).astype(BF16)
        bend = b[C - 1:C, :]
        kdec = (k * jnp.exp(bend - b)).astype(BF16)
        lhs_parts = []
        kt_parts = []
        for i in range(NSUB):
            in_sub = (row_k >> _shift(SUB)) == i
            lhs_parts.append(jnp.where(in_sub, qt, 0.0).astype(BF16))
            if i == 0:
                arg = jnp.where(row_k < SUB, -b, NEG_BIG)
            else:
                arg = jnp.where(row_k < SUB * (i + 1), b[SUB * i - 1:SUB * i, :] - b, NEG_BIG)
            kt_parts.append(k * jnp.exp(arg))
        lhs = jnp.concatenate(lhs_parts, axis=1)
        rt_rows = []
        for h in range(nh):
            rt_rows.append(jnp.concatenate(
                [jnp.where(head_k == h, kt, 0.0).astype(BF16) for kt in kt_parts], axis=1))
        rt = jnp.concatenate(rt_rows, axis=0)
        a_all = _dot_nt(lhs, rt)
        a_m = jnp.where(causal, a_all, 0.0).astype(BF16)
        vbd = jnp.concatenate([jnp.where(head_v == h, v, 0).astype(BF16) for h in range(nh)], axis=0)
        s_t = s_scr[...]
        o = _dot(a_m, vbd) + _dot_nt(qb, s_t.astype(BF16))
        o_ref[0, pl.ds(r0, C), :] = o
        upd = _dot_tn(v, kdec)
        s_scr[...] = s_t * jnp.exp(bend) + jnp.where(bd_mask, upd, 0.0)
        return carry

    lax.fori_loop(0, n_chunks, body, 0)

    @pl.when(pl.program_id(1) == pl.num_programs(1) - 1)
    def _():
        st_ref[0] = s_scr[...]


def _gla_prompt(q, k, g, v, dk, dv, name):
    B, L, hdk = q.shape
    hdv = v.shape[-1]
    nh = hdk // dk
    assert nh == hdv // dv and nh * CHUNK <= 256
    rows = 512 if L % 512 == 0 else L
    assert rows % CHUNK == 0 and L % rows == 0
    C = CHUNK
    t_full = np.tril(np.ones((C, C), np.float32))
    same_sub = (np.arange(C)[:, None] // SUB) == (np.arange(C)[None, :] // SUB)
    t_sub = t_full * same_sub
    tt = np.concatenate([np.tile(t_full, (1, 3)), np.tile(t_sub, (1, 3))], axis=0)
    tt = jnp.asarray(tt, BF16)
    blk = lambda w: pl.BlockSpec((1, rows, w), lambda b, t: (b, t, 0))
    o, st = pl.pallas_call(
        functools.partial(_gla_chunk_kernel, dk, dv, rows // C),
        grid=(B, L // rows),
        in_specs=[blk(hdk), blk(hdk), blk(hdk), blk(hdv),
                  pl.BlockSpec((2 * C, 3 * C), lambda b, t: (0, 0))],
        out_specs=[blk(hdv), pl.BlockSpec((1, hdv, hdk), lambda b, t: (b, 0, 0))],
        out_shape=[jax.ShapeDtypeStruct((B, L, hdv), F32),
                   jax.ShapeDtypeStruct((B, hdv, hdk), F32)],
        scratch_shapes=[pltpu.VMEM((hdv, hdk), F32)],
        compiler_params=_cparams(("parallel", "arbitrary")),
        name=name,
    )(q, k, g, v, tt)
    st5 = st.reshape(B, nh, dv, nh, dk)
    idx = jnp.arange(nh)
    state = st5[:, idx, :, idx, :]
    return o, jnp.transpose(state, (1, 0, 3, 2))


FOX_HG = 4


def _fox_prompt_kernel(q_ref, k_ref, v_ref, c_ref, o_ref, qm_scr, m_scr, l_scr, acc_scr):
    i = pl.program_id(2)
    j = pl.program_id(3)
    tq = q_ref.shape[1]
    tk = k_ref.shape[1]
    w = q_ref.shape[2]
    lane_head = _iota((tq, w), 1) >> _shift(FOX_HEAD_DIM)

    @pl.when(j == 0)
    def _():
        q = q_ref[0]
        for h in range(FOX_HG):
            qm_scr[h] = jnp.where(lane_head == h, q, 0).astype(BF16)
        m_scr[...] = jnp.full_like(m_scr, -jnp.inf)
        l_scr[...] = jnp.zeros_like(l_scr)
        acc_scr[...] = jnp.zeros_like(acc_scr)

    def compute(masked):
        k4 = k_ref[0].astype(BF16)
        v4 = v_ref[0].astype(BF16)
        kv_head = _iota((tk, w), 1) >> _shift(FOX_HEAD_DIM)
        if masked:
            keep = _iota((tq, tk), 1) <= _iota((tq, tk), 0)
        pv = jnp.zeros((tq, w), F32)
        alpha_x = jnp.zeros((tq, w), F32)
        for h in range(FOX_HG):
            s = _dot_nt(qm_scr[h], k4) - c_ref[0, 0, h:h + 1, :]
            if masked:
                s = jnp.where(keep, s, -jnp.inf)
            m_prev = m_scr[h]
            m_new = jnp.maximum(m_prev, jnp.max(s, axis=1, keepdims=True))
            alpha = jnp.exp(m_prev - m_new)
            p = jnp.exp(s - m_new)
            l_scr[h] = alpha * l_scr[h] + jnp.sum(p, axis=1, keepdims=True)
            m_scr[h] = m_new
            vh = jnp.where(kv_head == h, v4, 0).astype(BF16)
            pv = pv + _dot(p.astype(BF16), vh)
            alpha_x = jnp.where(lane_head == h, alpha, alpha_x)
        acc_scr[...] = acc_scr[...] * alpha_x + pv

    @pl.when(j < i)
    def _():
        compute(False)

    @pl.when(j == i)
    def _():
        compute(True)
        l_x = jnp.zeros((tq, w), F32)
        for h in range(FOX_HG):
            l_x = jnp.where(lane_head == h, l_scr[h], l_x)
        o_ref[0] = acc_scr[...] / l_x


def _fox_prompt(q, k, v, ct, tile):
    B, L, W = q.shape
    gw = FOX_HG * FOX_HEAD_DIM
    n_hg = W // gw
    nq = L // tile
    c4 = ct.reshape(B, n_hg, FOX_HG, L)
    qspec = pl.BlockSpec((1, tile, gw), lambda b, g, i, j: (b, i, g))
    kspec = pl.BlockSpec((1, tile, gw), lambda b, g, i, j: (b, jnp.minimum(i, j), g))
    cspec = pl.BlockSpec((1, 1, FOX_HG, tile), lambda b, g, i, j: (b, g, 0, jnp.minimum(i, j)))
    return pl.pallas_call(
        _fox_prompt_kernel,
        grid=(B, n_hg, nq, nq),
        in_specs=[qspec, kspec, kspec, cspec],
        out_specs=qspec,
        out_shape=jax.ShapeDtypeStruct((B, L, W), F32),
        scratch_shapes=[pltpu.VMEM((FOX_HG, tile, gw), BF16),
                        pltpu.VMEM((FOX_HG, tile, 1), F32),
                        pltpu.VMEM((FOX_HG, tile, 1), F32),
                        pltpu.VMEM((tile, gw), F32)],
        compiler_params=_cparams(("parallel", "parallel", "parallel", "arbitrary")),
        name="fox_prompt",
    )(q, k, v, c4)


def _head_norm_gate(o, gate, w_row, gsum):
    y = o * o
    y1 = y.astype(BF16)
    y2 = (y - y1.astype(F32)).astype(BF16)
    ms = _dot(jnp.concatenate([y1, y2], axis=1), gsum)
    return o * lax.rsqrt(ms + RMS_EPS) * w_row * (gate * _sigmoid(gate))


def _merge_kernel(x_ref, ohg_ref, ghg_ref, ofox_ref, ogla_ref, ggla_ref, hgn_ref, glan_ref,
                  gsum_ref, wout_ref, ffn_norm_ref, x1_ref, h2_ref):
    gsum = gsum_ref[...]
    hg = _head_norm_gate(ohg_ref[...], ghg_ref[...], hgn_ref[...], gsum)
    gla = _head_norm_gate(ogla_ref[...], ggla_ref[...], glan_ref[...], gsum)
    mix = jnp.concatenate([hg.astype(BF16), ofox_ref[...].astype(BF16), gla.astype(BF16)], axis=1)
    x1 = x_ref[...] + _dot(mix, wout_ref[...])
    x1_ref[...] = x1
    h2_ref[...] = _rms(x1, ffn_norm_ref[...]).astype(BF16)


def _merge(x, ohg, ghg, ofox, ogla, ggla, layer, tm, pw):
    M, D = x.shape
    row = lambda w: pl.BlockSpec((tm, w), lambda t: (t, 0))
    const = lambda shp: pl.BlockSpec(shp, lambda t: (0,) * len(shp))
    return pl.pallas_call(
        _merge_kernel,
        grid=(M // tm,),
        in_specs=[row(D), row(256), row(256), row(512), row(256), row(256),
                  const((1, 256)), const((1, 256)), const((512, 256)), const((D, D)), const((1, D))],
        out_specs=[row(D), row(D)],
        out_shape=[jax.ShapeDtypeStruct((M, D), F32), jax.ShapeDtypeStruct((M, D), BF16)],
        compiler_params=_cparams(("parallel",)),
        name="merge",
    )(x, ohg, ghg, ofox, ogla, ggla, pw["hgn"][layer], pw["glan"][layer], pw["gsum"],
      pw["w_out"][layer], pw["ffn_norm"][layer])


def _ffn_kernel(final, h_ref, x1_ref, wg_ref, wu_ref, wd_ref, fn_ref, o_ref, acc_ref):
    j = pl.program_id(1)

    @pl.when(j == 0)
    def _():
        acc_ref[...] = x1_ref[...]

    h = h_ref[...]
    gate = _dot(h, wg_ref[...])
    up = _dot(h, wu_ref[...])
    act = (gate * _sigmoid(gate) * up).astype(BF16)
    acc_ref[...] += _dot(act, wd_ref[...])

    @pl.when(j == pl.num_programs(1) - 1)
    def _():
        x2 = acc_ref[...]
        o_ref[...] = _rms(x2, fn_ref[...]) if final else x2


def _ffn(h2, x1, layer, final, tm, pw):
    M, D = x1.shape
    ff = pw["w_down"][layer].shape[0]
    tf = pw["ff_tile"]
    nf = ff // tf
    return pl.pallas_call(
        functools.partial(_ffn_kernel, final),
        grid=(M // tm, nf),
        in_specs=[pl.BlockSpec((tm, D), lambda t, j: (t, 0)),
                  pl.BlockSpec((tm, D), lambda t, j: (t, 0)),
                  pl.BlockSpec((D, tf), lambda t, j: (0, j)),
                  pl.BlockSpec((D, tf), lambda t, j: (0, j + nf)),
                  pl.BlockSpec((tf, D), lambda t, j: (j, 0)),
                  pl.BlockSpec((1, D), lambda t, j: (0, 0))],
        out_specs=pl.BlockSpec((tm, D), lambda t, j: (t, 0)),
        out_shape=jax.ShapeDtypeStruct((M, D), F32),
        scratch_shapes=[pltpu.VMEM((tm, D), F32)],
        compiler_params=_cparams(("parallel", "arbitrary")),
        name="ffn",
    )(h2, x1, pw["w_gu"][layer], pw["w_gu"][layer], pw["w_down"][layer], pw["final_norm"])


def _lin_step_kernel(s_ref, q_ref, k_ref, g_ref, v_ref, o_ref, sn_ref):
    nh = s_ref.shape[1]
    for h in range(nh):
        s_new = jnp.exp(g_ref[0, h]) * s_ref[0, h] + k_ref[0, h] * v_ref[0, h]
        sn_ref[0, h] = s_new
        o_ref[0, h] = jnp.sum(q_ref[0, h] * s_new, axis=0, keepdims=True)


def _lin_step(state, q, k, g, v, name):
    B, nh, dk, dv = state.shape
    col = lambda a: a.reshape(B, nh, dk, 1)
    cspec = pl.BlockSpec((1, nh, dk, 1), lambda b: (b, 0, 0, 0))
    sspec = pl.BlockSpec((1, nh, dk, dv), lambda b: (b, 0, 0, 0))
    vspec = pl.BlockSpec((1, nh, 1, dv), lambda b: (b, 0, 0, 0))
    o, sn = pl.pallas_call(
        _lin_step_kernel,
        grid=(B,),
        in_specs=[sspec, cspec, cspec, cspec, vspec],
        out_specs=[vspec, sspec],
        out_shape=[jax.ShapeDtypeStruct((B, nh, 1, dv), F32),
                   jax.ShapeDtypeStruct((B, nh, dk, dv), F32)],
        compiler_params=_cparams(("parallel",)),
        name=name,
    )(state, col(q), col(k), col(g), v.reshape(B, nh, 1, dv))
    return o.reshape(B, nh * dv), sn


PAGES_PER_STEP = 8


def _fox_decode_kernel(npg, pt_ref, *refs):
    qcol_ref, knew_ref, vnew_ref, lfnew_ref, u_ref, e_ref = refs[:6]
    k_refs = refs[6:6 + npg]
    v_refs = refs[6 + npg:6 + 2 * npg]
    lf_refs = refs[6 + 2 * npg:6 + 3 * npg]
    o_ref = refs[6 + 3 * npg]
    qbd_scr, m_scr, l_scr, t_scr, acc_scr = refs[7 + 3 * npg:]
    j = pl.program_id(1)
    w = knew_ref.shape[-1]
    nh = lfnew_ref.shape[-1]
    hd = w // nh
    lane_head = _iota((1, w), 1) >> _shift(hd)

    def expand(r):
        out = jnp.zeros((1, w), F32)
        for h in range(nh):
            out = jnp.where(lane_head == h, r[:, h:h + 1], out)
        return out

    @pl.when(j == 0)
    def _():
        bd = (_iota((w, nh), 0) >> _shift(hd)) == _iota((w, nh), 1)
        qbd = jnp.where(bd, qcol_ref[0], 0.0).astype(BF16)
        qbd_scr[...] = qbd
        kn = jnp.broadcast_to(knew_ref[0], (8, w)).astype(BF16)
        m_scr[...] = _dot(kn, qbd)[0:1, :]
        l_scr[...] = jnp.ones_like(l_scr)
        t_scr[...] = lfnew_ref[0]
        acc_scr[...] = jnp.where(_iota((8, w), 0) == 0, jnp.broadcast_to(vnew_ref[0], (8, w)), 0.0)

    qbd = qbd_scr[...]
    u = u_ref[...]
    t_run = t_scr[...]
    scores = [None] * npg
    for i in reversed(range(npg)):
        x = lf_refs[i][...]
        x1, x2, x3 = _split3(x)
        y = _dot(u, x1) + _dot(u, x2) + _dot(u, x3)
        scores[i] = _dot(k_refs[i][...].astype(BF16), qbd) + (y + t_run)
        t_run = t_run + (y[0:1, :] + x[0:1, :])
    t_scr[...] = t_run
    m_prev = m_scr[...]
    m_new = m_prev
    for i in range(npg):
        m_new = jnp.maximum(m_new, jnp.max(scores[i], axis=0, keepdims=True))
    alpha = jnp.exp(m_prev - m_new)
    l_new = alpha * l_scr[...]
    contrib = jnp.zeros((8, w), F32)
    for i in range(npg):
        p = jnp.exp(scores[i] - m_new)
        l_new = l_new + jnp.sum(p, axis=0, keepdims=True)
        pexp = _dot(p.astype(BF16), e_ref[...])
        pv = pexp * v_refs[i][...]
        contrib = contrib + jnp.sum(pv.reshape(pv.shape[0] // 8, 8, w), axis=0)
    m_scr[...] = m_new
    l_scr[...] = l_new
    acc_scr[...] = acc_scr[...] * expand(alpha) + contrib

    @pl.when(j == pl.num_programs(1) - 1)
    def _():
        o_ref[0] = jnp.sum(acc_scr[...], axis=0, keepdims=True) / expand(l_scr[...])


def _fox_decode(q, knew, vnew, lfnew, cache_k, cache_v, cache_lf, page_table, layer):
    B, W = q.shape
    nh = lfnew.shape[-1]
    depth, n_pool, page = cache_k.shape[:3]
    n_pages = page_table.shape[1]
    npg = PAGES_PER_STEP
    assert n_pages % npg == 0
    ng = n_pages // npg
    ck = cache_k.reshape(depth, n_pool, page, W)
    cv = cache_v.reshape(depth, n_pool, page, W)
    u = jnp.asarray(np.triu(np.ones((page, page), np.float32), 1), BF16)
    e = jnp.asarray(np.repeat(np.eye(nh, dtype=np.float32), W // nh, axis=1), BF16)
    pt_flat = page_table.reshape(-1)

    def page_map(i):
        def index_map(b, j, pt):
            return (layer, pt[b * n_pages + (ng - 1 - j) * npg + i], 0, 0)
        return index_map

    per_b = lambda shp: pl.BlockSpec((1,) + shp, lambda b, j, pt: (b, 0, 0))
    const = lambda shp: pl.BlockSpec(shp, lambda b, j, pt: (0, 0))
    in_specs = [per_b((W, 1)), per_b((1, W)), per_b((1, W)), per_b((1, nh)),
                const((page, page)), const((nh, W))]
    in_specs += [pl.BlockSpec((None, None, page, W), page_map(i)) for i in range(npg)]
    in_specs += [pl.BlockSpec((None, None, page, W), page_map(i)) for i in range(npg)]
    in_specs += [pl.BlockSpec((None, None, page, nh), page_map(i)) for i in range(npg)]
    out = pl.pallas_call(
        functools.partial(_fox_decode_kernel, npg),
        grid_spec=pltpu.PrefetchScalarGridSpec(
            num_scalar_prefetch=1,
            grid=(B, ng),
            in_specs=in_specs,
            out_specs=pl.BlockSpec((1, 1, W), lambda b, j, pt: (b, 0, 0)),
            scratch_shapes=[pltpu.VMEM((W, nh), BF16), pltpu.VMEM((1, nh), F32),
                            pltpu.VMEM((1, nh), F32), pltpu.VMEM((1, nh), F32),
                            pltpu.VMEM((8, W), F32)]),
        out_shape=jax.ShapeDtypeStruct((B, 1, W), F32),
        compiler_params=_cparams(("parallel", "arbitrary")),
        name="fox_decode",
    )(pt_flat, q.reshape(B, W, 1), knew.reshape(B, 1, W), vnew.reshape(B, 1, W),
      lfnew.reshape(B, 1, nh), u, e, *([ck] * npg), *([cv] * npg), *([cache_lf] * npg))
    return out.reshape(B, W)


def _prepare(attn_norm, w_in, fox_f_bias, gla_gate_w2, gla_gate_b, hgrn_lb_logits, hgrn_out_norm,
             gla_out_norm, w_out, ffn_norm, w_gate_up, w_down, final_norm):
    depth, D, in_width = w_in.shape
    n_fh = fox_f_bias.shape[1]
    main = 3328
    off_ff = 2560
    off_glr = in_width - GLA_GATE_RANK
    assert off_glr == off_ff + n_fh + (main - off_ff)
    w_main = jnp.concatenate([w_in[:, :, :off_ff], w_in[:, :, off_ff + n_fh:off_glr]], axis=2)
    misc = jnp.concatenate([w_in[:, :, off_ff:off_ff + n_fh], w_in[:, :, off_glr:],
                            jnp.zeros((depth, D, LANES - n_fh - GLA_GATE_RANK), w_in.dtype)], axis=2)
    w_p = jnp.concatenate([w_main, misc], axis=2).astype(BF16)
    assert w_p.shape[2] == W_PAD
    fbias = jnp.zeros((depth, 1, LANES), F32).at[:, 0, :n_fh].set(fox_f_bias)
    w2pad = jnp.zeros((depth, LANES, LANES), F32).at[:, n_fh:n_fh + GLA_GATE_RANK, :].set(gla_gate_w2)
    ff = w_down.shape[1]
    tf = ff // 2 if (ff // 2) % LANES == 0 else ff
    n_hg_heads = 256 // HG_DV
    gsum = np.kron(np.eye(n_hg_heads, dtype=np.float32), np.full((HG_DV, HG_DV), 1.0 / HG_DV, np.float32))
    return dict(
        n_fox_heads=n_fh,
        attn_norm=attn_norm.reshape(depth, 1, D),
        w_in=w_p,
        fbias=fbias,
        lb_logits=hgrn_lb_logits,
        w2pad=w2pad.astype(BF16),
        gla_b=gla_gate_b.reshape(depth, 1, LANES),
        hgn=jnp.tile(hgrn_out_norm, (1, n_hg_heads)).reshape(depth, 1, 256),
        glan=jnp.tile(gla_out_norm, (1, 256 // GLA_DV)).reshape(depth, 1, 256),
        gsum=jnp.asarray(np.concatenate([gsum, gsum], axis=0), BF16),
        w_out=w_out.astype(BF16),
        ffn_norm=ffn_norm.reshape(depth, 1, D),
        w_gu=w_gate_up.astype(BF16),
        w_down=w_down.astype(BF16),
        ff_tile=tf,
        final_norm=final_norm.reshape(1, D),
    )


def kernel(x_prompt, x_sample, cache_fox_k, cache_fox_v, cache_fox_logf, state_hgrn, state_gla, page_table, attn_norm, w_in, fox_f_bias, gla_gate_w2, gla_gate_b, hgrn_lb_logits, hgrn_out_norm, gla_out_norm, w_out, ffn_norm, w_gate_up, w_down, final_norm):
    B, L, D = x_prompt.shape
    DB = x_sample.shape[0]
    depth = w_in.shape[0]
    n_fh = fox_f_bias.shape[1]
    pw = _prepare(attn_norm, w_in, fox_f_bias, gla_gate_w2, gla_gate_b, hgrn_lb_logits,
                  hgrn_out_norm, gla_out_norm, w_out, ffn_norm, w_gate_up, w_down, final_norm)
    tm = 512
    xp = x_prompt
    xs = x_sample.reshape(1, DB, D)
    kp, vp, lfp, hsp, gsp = [], [], [], [], []
    ksm, vsm, lfs, hss, gss = [], [], [], [], []
    for l in range(depth):
        final = l == depth - 1
        (hq, hk, hlf, hv, hgate, fq, fk, fv, flf, gq, gk, gv, glg, ggate, ct) = _project(
            xp, l, True, tm, BF16, pw)
        o_hg, s_hg = _gla_prompt(hq, hk, hlf, hv, HG_DK, HG_DV, "hgrn_prompt")
        o_fox = _fox_prompt(fq, fk, fv, ct, tm)
        o_gla, s_gla = _gla_prompt(gq, gk, glg, gv, GLA_DK, GLA_DV, "gla_prompt")
        M = B * L
        r2 = lambda a: a.reshape(M, a.shape[-1])
        x1, h2 = _merge(r2(xp), r2(o_hg), r2(hgate), r2(o_fox), r2(o_gla), r2(ggate), l, tm, pw)
        xp = _ffn(h2, x1, l, final, tm, pw).reshape(B, L, D)
        kp.append(fk.reshape(B, L, n_fh, FOX_HEAD_DIM))
        vp.append(fv.reshape(B, L, n_fh, FOX_HEAD_DIM))
        lfp.append(flf)
        hsp.append(s_hg)
        gsp.append(s_gla)
        (hq, hk, hlf, hv, hgate, fq, fk, fv, flf, gq, gk, gv, glg, ggate) = _project(
            xs, l, False, DB, F32, pw)
        s2 = lambda a: a.reshape(DB, a.shape[-1])
        o_hg, s_hg = _lin_step(state_hgrn[l], s2(hq), s2(hk), s2(hlf), s2(hv), "hgrn_step")
        o_fox = _fox_decode(s2(fq).astype(F32), s2(fk), s2(fv), s2(flf), cache_fox_k, cache_fox_v,
                            cache_fox_logf, page_table, l)
        o_gla, s_gla = _lin_step(state_gla[l], s2(gq), s2(gk), s2(glg), s2(gv), "gla_step")
        x1, h2 = _merge(s2(xs), o_hg, s2(hgate), o_fox, o_gla, s2(ggate), l, DB, pw)
        xs = _ffn(h2, x1, l, final, DB, pw).reshape(1, DB, D)
        ksm.append(fk.reshape(DB, 1, n_fh, FOX_HEAD_DIM))
        vsm.append(fv.reshape(DB, 1, n_fh, FOX_HEAD_DIM))
        lfs.append(flf.reshape(DB, 1, n_fh))
        hss.append(s_hg)
        gss.append(s_gla)
    return (xp, xs.reshape(DB, 1, D),
            jnp.stack(kp), jnp.stack(vp), jnp.stack(lfp), jnp.stack(hsp), jnp.stack(gsp),
            jnp.stack(ksm), jnp.stack(vsm), jnp.stack(lfs), jnp.stack(hss), jnp.stack(gss))
```

```python
import functools

import numpy as np
import jax
import jax.numpy as jnp
from jax import lax
from jax.experimental import pallas as pl
from jax.experimental.pallas import tpu as pltpu

F32 = jnp.float32
BF16 = jnp.bfloat16

HG_DK = 64
HG_DV = 64
FOX_HEAD_DIM = 64
GLA_DK = 32
GLA_DV = 64
GLA_GATE_RANK = 16
GLA_GATE_NORMALIZER = 16.0
RMS_EPS = 1e-6

LANES = 128
CHUNK = 64
SUB = 16
NSUB = CHUNK // SUB
NEG_BIG = -1e30
VMEM_LIMIT = 48 * 1024 * 1024


def _cparams(sem):
    return pltpu.CompilerParams(dimension_semantics=sem, vmem_limit_bytes=VMEM_LIMIT)


def _shift(n):
    s = int(np.log2(n))
    assert (1 << s) == n
    return s


def _iota(shape, dim):
    return lax.broadcasted_iota(jnp.int32, shape, dim)


def _split3(x):
    x1 = x.astype(BF16)
    r1 = x - x1.astype(F32)
    x2 = r1.astype(BF16)
    r2 = r1 - x2.astype(F32)
    return x1, x2, r2.astype(BF16)


def _log_sigmoid(x):
    return jnp.minimum(x, 0.0) - jnp.log1p(jnp.exp(-jnp.abs(x)))


def _sigmoid(x):
    return 1.0 / (1.0 + jnp.exp(-x))


def _dot(a, b):
    return jnp.dot(a, b, preferred_element_type=F32)


def _dot_nt(a, b):
    return lax.dot_general(a, b, (((1,), (1,)), ((), ())), preferred_element_type=F32)


def _dot_tn(a, b):
    return lax.dot_general(a, b, (((0,), (0,)), ((), ())), preferred_element_type=F32)


def _rms(x, w):
    ms = jnp.mean(x * x, axis=-1, keepdims=True)
    return x * lax.rsqrt(ms + RMS_EPS) * w


W_PAD = 3456
MISC = 3328
FOX_W = 512
COMMON_OUT = ("hq", "hk", "hlf", "hv", "hgate", "fq", "flf", "gq", "gk", "gv", "glg", "ggate")
PROMPT_OUT = COMMON_OUT + ("fk_b", "fkt", "fvt", "fvt_b", "c")
SAMPLE_OUT = COMMON_OUT + ("fk", "fv")


def _proj_kernel(layer, prompt, *refs):
    (x_ref, norm_ref, w_ref, fbias_ref, lbl_ref, w2_ref, gb_ref) = refs[:7]
    k = 7
    if prompt:
        wkvt_ref, tri_ref = refs[7], refs[8]
        k = 9
    names = PROMPT_OUT if prompt else SAMPLE_OUT
    o = dict(zip(names, refs[k:k + len(names)]))
    if prompt:
        carry_ref = refs[k + len(names)]

    hb = _rms(x_ref[0], norm_ref[...]).astype(BF16)

    def seg(a, b):
        return _dot(hb, w_ref[:, a:b])

    lbl = lbl_ref[...]
    e = jnp.exp(lbl - jnp.max(lbl, axis=0, keepdims=True))
    sm = e / jnp.sum(e, axis=0, keepdims=True)
    lb = jnp.zeros((1, lbl.shape[1]), F32)
    for i in range(1, layer + 1):
        lb = lb + sm[i:i + 1, :]
    o["hq"][0] = seg(0, 256)
    f = lb + (1.0 - lb) * _sigmoid(seg(256, 512))
    o["hlf"][0] = jnp.log(f)
    o["hk"][0] = 1.0 - f
    o["hv"][0] = seg(512, 768).astype(o["hv"].dtype)
    o["hgate"][0] = seg(768, 1024)
    o["fq"][0] = (seg(1024, 1536) * (FOX_HEAD_DIM ** -0.5)).astype(BF16)
    if prompt:
        o["fk_b"][0] = seg(1536, 2048).astype(BF16)
        kvt = _dot_nt(wkvt_ref[...], hb)
        o["fkt"][0] = kvt[0:FOX_W]
        o["fvt"][0] = kvt[FOX_W:2 * FOX_W]
        o["fvt_b"][0] = kvt[FOX_W:2 * FOX_W].astype(BF16)
    else:
        o["fk"][0] = seg(1536, 2048)
        o["fv"][0] = seg(2048, 2560)
    o["gq"][0] = seg(2560, 2688) * (GLA_DK ** -0.5)
    o["gk"][0] = seg(2688, 2816)
    o["gv"][0] = seg(2816, 3072).astype(o["gv"].dtype)
    o["ggate"][0] = seg(3072, 3328)
    misc = seg(MISC, W_PAD)
    lf_all = _log_sigmoid(misc + fbias_ref[...])
    n_fh = o["flf"].shape[-1]
    o["flf"][0] = lf_all[:, 0:n_fh]
    g_logit = _dot(misc.astype(BF16), w2_ref[...]) + gb_ref[...]
    o["glg"][0] = _log_sigmoid(g_logit) / GLA_GATE_NORMALIZER

    if prompt:
        @pl.when(pl.program_id(1) == 0)
        def _():
            carry_ref[...] = jnp.zeros_like(carry_ref)

        x1, x2, x3 = _split3(lf_all)
        tri = tri_ref[...]
        c_all = _dot(tri, x1) + _dot(tri, x2) + _dot(tri, x3) + carry_ref[...]
        o["c"][0] = c_all[:, 0:n_fh]
        tm = c_all.shape[0]
        carry_ref[...] = c_all[tm - 1:tm, :]


def _project(x3, layer, prompt, tm, pw):
    B, L, D = x3.shape
    n_fh = pw["n_fox_heads"]
    grid = (B, L // tm)
    row = lambda w: pl.BlockSpec((1, tm, w), lambda b, t: (b, t, 0))
    colm = pl.BlockSpec((1, FOX_W, tm), lambda b, t: (b, 0, t))
    const = lambda shp: pl.BlockSpec(shp, lambda b, t: (0,) * len(shp))
    v_dtype = BF16 if prompt else F32
    rows = dict(hq=(256, F32), hk=(256, F32), hlf=(256, F32), hv=(256, v_dtype), hgate=(256, F32),
                fq=(FOX_W, BF16), flf=(n_fh, F32), gq=(128, F32), gk=(128, F32), gv=(256, v_dtype),
                glg=(128, F32), ggate=(256, F32), fk_b=(FOX_W, BF16), c=(n_fh, F32),
                fk=(FOX_W, F32), fv=(FOX_W, F32))
    cols = dict(fkt=F32, fvt=F32, fvt_b=BF16)
    in_arrays = [x3, pw["attn_norm"][layer], pw["w_in"][layer], pw["fbias"][layer],
                 pw["lb_logits"], pw["w2pad"][layer], pw["gla_b"][layer]]
    in_specs = [row(D), const((1, D)), const((D, W_PAD)), const((1, LANES)),
                const(pw["lb_logits"].shape), const((LANES, LANES)), const((1, LANES))]
    scratch = []
    if prompt:
        tri = jnp.asarray(np.tril(np.ones((tm, tm), np.float32)), BF16)
        in_arrays += [pw["w_kvt"][layer], tri]
        in_specs += [const((2 * FOX_W, D)), const((tm, tm))]
        scratch.append(pltpu.VMEM((1, LANES), F32))
    names = PROMPT_OUT if prompt else SAMPLE_OUT
    out_shape, out_specs = [], []
    for n in names:
        if n in cols:
            out_shape.append(jax.ShapeDtypeStruct((B, FOX_W, L), cols[n]))
            out_specs.append(colm)
        else:
            w, dt = rows[n]
            out_shape.append(jax.ShapeDtypeStruct((B, L, w), dt))
            out_specs.append(row(w))
    outs = pl.pallas_call(
        functools.partial(_proj_kernel, layer, prompt),
        grid=grid, in_specs=in_specs, out_specs=out_specs, out_shape=out_shape,
        scratch_shapes=scratch,
        compiler_params=_cparams(("parallel", "arbitrary")),
        name="proj_prompt" if prompt else "proj_sample",
    )(*in_arrays)
    return dict(zip(names, outs))


def _gla_chunk_kernel(dk, dv, n_chunks, q_ref, k_ref, g_ref, v_ref, tt_ref, o_ref, st_ref, s_scr):
    hdk = q_ref.shape[-1]
    hdv = v_ref.shape[-1]
    nh = hdk // dk
    C = CHUNK

    @pl.when(pl.program_id(1) == 0)
    def _():
        s_scr[...] = jnp.zeros_like(s_scr)

    row_k = _iota((C, hdk), 0)
    head_k = _iota((C, hdk), 1) >> _shift(dk)
    head_v = _iota((C, hdv), 1) >> _shift(dv)
    a_row = _iota((C, nh * C), 0)
    a_col = _iota((C, nh * C), 1) & (C - 1)
    causal = a_col <= a_row
    bd_mask = (_iota((hdv, hdk), 0) >> _shift(dv)) == (_iota((hdv, hdk), 1) >> _shift(dk))

    def body(ci, carry):
        r0 = pl.multiple_of(ci * C, C)
        q = q_ref[0, pl.ds(r0, C), :]
        k = k_ref[0, pl.ds(r0, C), :]
        g = g_ref[0, pl.ds(r0, C), :]
        v = v_ref[0, pl.ds(r0, C), :].astype(BF16)
        g1, g2, g3 = _split3(g)
        gcat = jnp.concatenate([g1, g2, g3], axis=0)
        bw = _dot(tt_ref[...], gcat)
        b = bw[0:C]
        w = bw[C:2 * C]
        qt = q * jnp.exp(w)
        qb = (q * jnp.exp(b)).astype(BF16)
        bend = b[C - 1:C, :]
        kdec = (k * jnp.exp(bend - b)).astype(BF16)
        lhs_parts = []
        kt_parts = []
        for i in range(NSUB):
            in_sub = (row_k >> _shift(SUB)) == i
            lhs_parts.append(jnp.where(in_sub, qt, 0.0).astype(BF16))
            if i == 0:
                arg = jnp.where(row_k < SUB, -b, NEG_BIG)
            else:
                arg = jnp.where(row_k < SUB * (i + 1), b[SUB * i - 1:SUB * i, :] - b, NEG_BIG)
            kt_parts.append(k * jnp.exp(arg))
        lhs = jnp.concatenate(lhs_parts, axis=1)
        rt_rows = []
        for h in range(nh):
            rt_rows.append(jnp.concatenate(
                [jnp.where(head_k == h, kt, 0.0).astype(BF16) for kt in kt_parts], axis=1))
        rt = jnp.concatenate(rt_rows, axis=0)
        a_all = _dot_nt(lhs, rt)
        a_m = jnp.where(causal, a_all, 0.0).astype(BF16)
        vbd = jnp.concatenate([jnp.where(head_v == h, v, 0).astype(BF16) for h in range(nh)], axis=0)
        s_t = s_scr[...]
        o = _dot(a_m, vbd) + _dot_nt(qb, s_t.astype(BF16))
        o_ref[0, pl.ds(r0, C), :] = o
        upd = _dot_tn(v, kdec)
        s_scr[...] = s_t * jnp.exp(bend) + jnp.where(bd_mask, upd, 0.0)
        return carry

    lax.fori_loop(0, n_chunks, body, 0)

    @pl.when(pl.program_id(1) == pl.num_programs(1) - 1)
    def _():
        st_ref[0] = s_scr[...]


def _gla_prompt(q, k, g, v, dk, dv, name):
    B, L, hdk = q.shape
    hdv = v.shape[-1]
    nh = hdk // dk
    assert nh == hdv // dv and nh * CHUNK <= 256
    rows = 512 if L % 512 == 0 else L
    assert rows % CHUNK == 0 and L % rows == 0
    C = CHUNK
    t_full = np.tril(np.ones((C, C), np.float32))
    same_sub = (np.arange(C)[:, None] // SUB) == (np.arange(C)[None, :] // SUB)
    t_sub = t_full * same_sub
    tt = np.concatenate([np.tile(t_full, (1, 3)), np.tile(t_sub, (1, 3))], axis=0)
    tt = jnp.asarray(tt, BF16)
    blk = lambda w: pl.BlockSpec((1, rows, w), lambda b, t: (b, t, 0))
    o, st = pl.pallas_call(
        functools.partial(_gla_chunk_kernel, dk, dv, rows // C),
        grid=(B, L // rows),
        in_specs=[blk(hdk), blk(hdk), blk(hdk), blk(hdv),
                  pl.BlockSpec((2 * C, 3 * C), lambda b, t: (0, 0))],
        out_specs=[blk(hdv), pl.BlockSpec((1, hdv, hdk), lambda b, t: (b, 0, 0))],
        out_shape=[jax.ShapeDtypeStruct((B, L, hdv), F32),
                   jax.ShapeDtypeStruct((B, hdv, hdk), F32)],
        scratch_shapes=[pltpu.VMEM((hdv, hdk), F32)],
        compiler_params=_cparams(("parallel", "arbitrary")),
        name=name,
    )(q, k, g, v, tt)
    st5 = st.reshape(B, nh, dv, nh, dk)
    idx = jnp.arange(nh)
    state = st5[:, idx, :, idx, :]
    return o, jnp.transpose(state, (1, 0, 3, 2))


FOX_HG = 4


def _fox_prompt_kernel(q_ref, k_ref, vt_ref, c_ref, o_ref, qm_scr, m_scr, l_scr, acc_scr):
    i = pl.program_id(2)
    j = pl.program_id(3)
    tq = q_ref.shape[1]
    tk = k_ref.shape[1]
    w = q_ref.shape[2]
    hd = FOX_HEAD_DIM

    @pl.when(j == 0)
    def _():
        q = q_ref[0]
        lane_head = _iota((tq, w), 1) >> _shift(hd)
        for h in range(FOX_HG):
            qm_scr[h] = jnp.where(lane_head == h, q, 0).astype(BF16)
        m_scr[...] = jnp.full_like(m_scr, -jnp.inf)
        l_scr[...] = jnp.zeros_like(l_scr)
        acc_scr[...] = jnp.zeros_like(acc_scr)

    def compute(masked):
        k4 = k_ref[0]
        c = c_ref[0, 0]
        if masked:
            keep = _iota((tk, tq), 0) <= _iota((tk, tq), 1)
        for h in range(FOX_HG):
            s = _dot_nt(k4, qm_scr[h]) - c[:, h:h + 1]
            if masked:
                s = jnp.where(keep, s, -jnp.inf)
            m_prev = m_scr[h]
            m_new = jnp.maximum(m_prev, jnp.max(s, axis=0, keepdims=True))
            alpha = jnp.exp(m_prev - m_new)
            p = jnp.exp(s - m_new)
            l_scr[h] = alpha * l_scr[h] + jnp.sum(p, axis=0, keepdims=True)
            m_scr[h] = m_new
            pv = _dot(vt_ref[0, h * hd:(h + 1) * hd, :], p.astype(BF16))
            acc_scr[h] = acc_scr[h] * alpha + pv

    @pl.when(j < i)
    def _():
        compute(False)

    @pl.when(j == i)
    def _():
        compute(True)
        ot = jnp.concatenate([acc_scr[h] / l_scr[h] for h in range(FOX_HG)], axis=0)
        o_ref[0] = ot.T


def _fox_prompt(q, k, vt, c, tile):
    B, L, W = q.shape
    n_heads = c.shape[-1]
    hd = FOX_HEAD_DIM
    gw = FOX_HG * hd
    n_hg = W // gw
    nq = L // tile
    qspec = pl.BlockSpec((1, tile, gw), lambda b, g, i, j: (b, i, g))
    kspec = pl.BlockSpec((1, tile, gw), lambda b, g, i, j: (b, jnp.minimum(i, j), g))
    vspec = pl.BlockSpec((1, gw, tile), lambda b, g, i, j: (b, g, jnp.minimum(i, j)))
    c4 = jnp.transpose(c.reshape(B, L, n_hg, FOX_HG), (0, 2, 1, 3))
    cspec = pl.BlockSpec((1, 1, tile, FOX_HG), lambda b, g, i, j: (b, g, jnp.minimum(i, j), 0))
    return pl.pallas_call(
        _fox_prompt_kernel,
        grid=(B, n_hg, nq, nq),
        in_specs=[qspec, kspec, vspec, cspec],
        out_specs=qspec,
        out_shape=jax.ShapeDtypeStruct((B, L, W), F32),
        scratch_shapes=[pltpu.VMEM((FOX_HG, tile, gw), BF16),
                        pltpu.VMEM((FOX_HG, 1, tile), F32),
                        pltpu.VMEM((FOX_HG, 1, tile), F32),
                        pltpu.VMEM((FOX_HG, hd, tile), F32)],
        compiler_params=_cparams(("parallel", "parallel", "parallel", "arbitrary")),
        name="fox_prompt",
    )(q, k, vt, c4)


def _head_norm_gate(o, gate, w_row, gsum):
    y = o * o
    y1 = y.astype(BF16)
    y2 = (y - y1.astype(F32)).astype(BF16)
    ms = _dot(jnp.concatenate([y1, y2], axis=1), gsum)
    return o * lax.rsqrt(ms + RMS_EPS) * w_row * (gate * _sigmoid(gate))


def _merge_kernel(x_ref, ohg_ref, ghg_ref, ofox_ref, ogla_ref, ggla_ref, hgn_ref, glan_ref,
                  gsum_ref, wout_ref, ffn_norm_ref, x1_ref, h2_ref):
    gsum = gsum_ref[...]
    hg = _head_norm_gate(ohg_ref[...], ghg_ref[...], hgn_ref[...], gsum)
    gla = _head_norm_gate(ogla_ref[...], ggla_ref[...], glan_ref[...], gsum)
    mix = jnp.concatenate([hg.astype(BF16), ofox_ref[...].astype(BF16), gla.astype(BF16)], axis=1)
    x1 = x_ref[...] + _dot(mix, wout_ref[...])
    x1_ref[...] = x1
    h2_ref[...] = _rms(x1, ffn_norm_ref[...]).astype(BF16)


def _merge(x, ohg, ghg, ofox, ogla, ggla, layer, tm, pw):
    M, D = x.shape
    row = lambda w: pl.BlockSpec((tm, w), lambda t: (t, 0))
    const = lambda shp: pl.BlockSpec(shp, lambda t: (0,) * len(shp))
    return pl.pallas_call(
        _merge_kernel,
        grid=(M // tm,),
        in_specs=[row(D), row(256), row(256), row(512), row(256), row(256),
                  const((1, 256)), const((1, 256)), const((512, 256)), const((D, D)), const((1, D))],
        out_specs=[row(D), row(D)],
        out_shape=[jax.ShapeDtypeStruct((M, D), F32), jax.ShapeDtypeStruct((M, D), BF16)],
        compiler_params=_cparams(("parallel",)),
        name="merge",
    )(x, ohg, ghg, ofox, ogla, ggla, pw["hgn"][layer], pw["glan"][layer], pw["gsum"],
      pw["w_out"][layer], pw["ffn_norm"][layer])


def _ffn_kernel(final, h_ref, x1_ref, wg_ref, wu_ref, wd_ref, fn_ref, o_ref, acc_ref):
    j = pl.program_id(1)

    @pl.when(j == 0)
    def _():
        acc_ref[...] = x1_ref[...]

    h = h_ref[...]
    gate = _dot(h, wg_ref[...])
    up = _dot(h, wu_ref[...])
    act = (gate * _sigmoid(gate) * up).astype(BF16)
    acc_ref[...] += _dot(act, wd_ref[...])

    @pl.when(j == pl.num_programs(1) - 1)
    def _():
        x2 = acc_ref[...]
        o_ref[...] = _rms(x2, fn_ref[...]) if final else x2


def _ffn(h2, x1, layer, final, tm, pw):
    M, D = x1.shape
    ff = pw["w_down"][layer].shape[0]
    tf = pw["ff_tile"]
    nf = ff // tf
    return pl.pallas_call(
        functools.partial(_ffn_kernel, final),
        grid=(M // tm, nf),
        in_specs=[pl.BlockSpec((tm, D), lambda t, j: (t, 0)),
                  pl.BlockSpec((tm, D), lambda t, j: (t, 0)),
                  pl.BlockSpec((D, tf), lambda t, j: (0, j)),
                  pl.BlockSpec((D, tf), lambda t, j: (0, j + nf)),
                  pl.BlockSpec((tf, D), lambda t, j: (j, 0)),
                  pl.BlockSpec((1, D), lambda t, j: (0, 0))],
        out_specs=pl.BlockSpec((tm, D), lambda t, j: (t, 0)),
        out_shape=jax.ShapeDtypeStruct((M, D), F32),
        scratch_shapes=[pltpu.VMEM((tm, D), F32)],
        compiler_params=_cparams(("parallel", "arbitrary")),
        name="ffn",
    )(h2, x1, pw["w_gu"][layer], pw["w_gu"][layer], pw["w_down"][layer], pw["final_norm"])


def _lin_step_kernel(s_ref, q_ref, k_ref, g_ref, v_ref, o_ref, sn_ref):
    nh = s_ref.shape[1]
    for h in range(nh):
        s_new = jnp.exp(g_ref[0, h]) * s_ref[0, h] + k_ref[0, h] * v_ref[0, h]
        sn_ref[0, h] = s_new
        o_ref[0, h] = jnp.sum(q_ref[0, h] * s_new, axis=0, keepdims=True)


def _lin_step(state, q, k, g, v, name):
    B, nh, dk, dv = state.shape
    col = lambda a: a.reshape(B, nh, dk, 1)
    cspec = pl.BlockSpec((1, nh, dk, 1), lambda b: (b, 0, 0, 0))
    sspec = pl.BlockSpec((1, nh, dk, dv), lambda b: (b, 0, 0, 0))
    vspec = pl.BlockSpec((1, nh, 1, dv), lambda b: (b, 0, 0, 0))
    o, sn = pl.pallas_call(
        _lin_step_kernel,
        grid=(B,),
        in_specs=[sspec, cspec, cspec, cspec, vspec],
        out_specs=[vspec, sspec],
        out_shape=[jax.ShapeDtypeStruct((B, nh, 1, dv), F32),
                   jax.ShapeDtypeStruct((B, nh, dk, dv), F32)],
        compiler_params=_cparams(("parallel",)),
        name=name,
    )(state, col(q), col(k), col(g), v.reshape(B, nh, 1, dv))
    return o.reshape(B, nh * dv), sn


PAGES_PER_STEP = 8


def _fox_decode_kernel(npg, pt_ref, *refs):
    qrow_ref, krow_ref, qcol_ref, vcol_ref, lfnew_ref, u_ref = refs[:6]
    k_refs = refs[6:6 + npg]
    v_refs = refs[6 + npg:6 + 2 * npg]
    lf_refs = refs[6 + 2 * npg:6 + 3 * npg]
    o_ref = refs[6 + 3 * npg]
    qb_scr, m_scr, l_scr, t_scr, acc_scr, s_scr, p_scr = refs[7 + 3 * npg:]
    j = pl.program_id(1)
    nh, hd, page = qb_scr.shape

    @pl.when(j == 0)
    def _():
        qb_scr[...] = jnp.broadcast_to(qcol_ref[0], (nh, hd, page))
        m_scr[...] = jnp.sum(qrow_ref[0] * krow_ref[0], axis=1, keepdims=True)
        l_scr[...] = jnp.ones_like(l_scr)
        t_scr[...] = lfnew_ref[0]
        acc_scr[...] = jnp.where(_iota((nh, hd, page), 2) == 0,
                                 jnp.broadcast_to(vcol_ref[0], (nh, hd, page)), 0.0)

    u = u_ref[...]
    t_run = t_scr[...]
    for i in reversed(range(npg)):
        x = lf_refs[i][...]
        x1, x2, x3 = _split3(x)
        cat = jnp.concatenate([x1.astype(F32), x2.astype(F32), x3.astype(F32),
                               jnp.zeros_like(x)], axis=0).astype(BF16)
        y4 = _dot(cat, u)
        y = y4[0:nh] + y4[nh:2 * nh] + y4[2 * nh:3 * nh]
        for h in range(nh):
            s_scr[i, h:h + 1, :] = jnp.sum(k_refs[i][h] * qb_scr[h], axis=0, keepdims=True)
        s_scr[i] = s_scr[i] + (y + t_run)
        t_run = t_run + (y[:, 0:1] + x[:, 0:1])
    t_scr[...] = t_run
    s_all = s_scr[...]
    m_prev = m_scr[...]
    m_new = jnp.maximum(m_prev, jnp.max(jnp.max(s_all, axis=0), axis=1, keepdims=True))
    alpha = jnp.exp(m_prev - m_new)
    p_all = jnp.exp(s_all - m_new)
    p_scr[...] = p_all
    l_scr[...] = alpha * l_scr[...] + jnp.sum(jnp.sum(p_all, axis=0), axis=1, keepdims=True)
    m_scr[...] = m_new
    for h in range(nh):
        contrib = jnp.zeros((hd, page), F32)
        for i in range(npg):
            contrib = contrib + p_scr[i, h:h + 1, :] * v_refs[i][h]
        acc_scr[h] = acc_scr[h] * alpha[h:h + 1, :] + contrib

    @pl.when(j == pl.num_programs(1) - 1)
    def _():
        l = l_scr[...]
        for h in range(nh):
            o_ref[0, h] = jnp.sum(acc_scr[h], axis=1, keepdims=True) / l[h:h + 1, :]


def _fox_decode(q, knew, vnew, lfnew, cache_k, cache_v, cache_lf, page_table, layer):
    B, W = q.shape
    depth, n_pool, page, nh, hd = cache_k.shape
    n_pages = page_table.shape[1]
    npg = PAGES_PER_STEP
    assert n_pages % npg == 0
    ng = n_pages // npg
    ck = jnp.transpose(cache_k, (0, 1, 3, 4, 2))
    cv = jnp.transpose(cache_v, (0, 1, 3, 4, 2))
    clf = jnp.transpose(cache_lf, (0, 1, 3, 2))
    u = jnp.asarray(np.tril(np.ones((page, page), np.float32), -1), BF16)
    pt_flat = page_table.reshape(-1)

    def page_map(nd):
        def make(i):
            def index_map(b, j, pt):
                return (layer, pt[b * n_pages + (ng - 1 - j) * npg + i]) + (0,) * nd
            return index_map
        return make

    per_b = lambda shp: pl.BlockSpec((1,) + shp, lambda b, j, pt: (b,) + (0,) * len(shp))
    in_specs = [per_b((nh, hd)), per_b((nh, hd)), per_b((nh, hd, 1)), per_b((nh, hd, 1)),
                per_b((nh, 1)), pl.BlockSpec((page, page), lambda b, j, pt: (0, 0))]
    in_specs += [pl.BlockSpec((None, None, nh, hd, page), page_map(3)(i)) for i in range(npg)]
    in_specs += [pl.BlockSpec((None, None, nh, hd, page), page_map(3)(i)) for i in range(npg)]
    in_specs += [pl.BlockSpec((None, None, nh, page), page_map(2)(i)) for i in range(npg)]
    out = pl.pallas_call(
        functools.partial(_fox_decode_kernel, npg),
        grid_spec=pltpu.PrefetchScalarGridSpec(
            num_scalar_prefetch=1,
            grid=(B, ng),
            in_specs=in_specs,
            out_specs=pl.BlockSpec((1, nh, hd, 1), lambda b, j, pt: (b, 0, 0, 0)),
            scratch_shapes=[pltpu.VMEM((nh, hd, page), F32), pltpu.VMEM((nh, 1), F32),
                            pltpu.VMEM((nh, 1), F32), pltpu.VMEM((nh, 1), F32),
                            pltpu.VMEM((nh, hd, page), F32),
                            pltpu.VMEM((npg, nh, page), F32), pltpu.VMEM((npg, nh, page), F32)]),
        out_shape=jax.ShapeDtypeStruct((B, nh, hd, 1), F32),
        compiler_params=_cparams(("parallel", "arbitrary")),
        name="fox_decode",
    )(pt_flat, q.reshape(B, nh, hd), knew.reshape(B, nh, hd), q.reshape(B, nh, hd, 1),
      vnew.reshape(B, nh, hd, 1), lfnew.reshape(B, nh, 1), u,
      *([ck] * npg), *([cv] * npg), *([clf] * npg))
    return out.reshape(B, W)


def _prepare(attn_norm, w_in, fox_f_bias, gla_gate_w2, gla_gate_b, hgrn_lb_logits, hgrn_out_norm,
             gla_out_norm, w_out, ffn_norm, w_gate_up, w_down, final_norm):
    depth, D, in_width = w_in.shape
    n_fh = fox_f_bias.shape[1]
    main = 3328
    off_ff = 2560
    off_glr = in_width - GLA_GATE_RANK
    assert off_glr == off_ff + n_fh + (main - off_ff)
    w_main = jnp.concatenate([w_in[:, :, :off_ff], w_in[:, :, off_ff + n_fh:off_glr]], axis=2)
    misc = jnp.concatenate([w_in[:, :, off_ff:off_ff + n_fh], w_in[:, :, off_glr:],
                            jnp.zeros((depth, D, LANES - n_fh - GLA_GATE_RANK), w_in.dtype)], axis=2)
    w_p = jnp.concatenate([w_main, misc], axis=2).astype(BF16)
    assert w_p.shape[2] == W_PAD
    w_kvt = jnp.transpose(w_in[:, :, 1536:1536 + 2 * FOX_W], (0, 2, 1)).astype(BF16)
    fbias = jnp.zeros((depth, 1, LANES), F32).at[:, 0, :n_fh].set(fox_f_bias)
    w2pad = jnp.zeros((depth, LANES, LANES), F32).at[:, n_fh:n_fh + GLA_GATE_RANK, :].set(gla_gate_w2)
    ff = w_down.shape[1]
    tf = ff // 2 if (ff // 2) % LANES == 0 else ff
    n_hg_heads = 256 // HG_DV
    gsum = np.kron(np.eye(n_hg_heads, dtype=np.float32), np.full((HG_DV, HG_DV), 1.0 / HG_DV, np.float32))
    return dict(
        n_fox_heads=n_fh,
        attn_norm=attn_norm.reshape(depth, 1, D),
        w_in=w_p,
        w_kvt=w_kvt,
        fbias=fbias,
        lb_logits=hgrn_lb_logits,
        w2pad=w2pad.astype(BF16),
        gla_b=gla_gate_b.reshape(depth, 1, LANES),
        hgn=jnp.tile(hgrn_out_norm, (1, n_hg_heads)).reshape(depth, 1, 256),
        glan=jnp.tile(gla_out_norm, (1, 256 // GLA_DV)).reshape(depth, 1, 256),
        gsum=jnp.asarray(np.concatenate([gsum, gsum], axis=0), BF16),
        w_out=w_out.astype(BF16),
        ffn_norm=ffn_norm.reshape(depth, 1, D),
        w_gu=w_gate_up.astype(BF16),
        w_down=w_down.astype(BF16),
        ff_tile=tf,
        final_norm=final_norm.reshape(1, D),
    )


def kernel(x_prompt, x_sample, cache_fox_k, cache_fox_v, cache_fox_logf, state_hgrn, state_gla, page_table, attn_norm, w_in, fox_f_bias, gla_gate_w2, gla_gate_b, hgrn_lb_logits, hgrn_out_norm, gla_out_norm, w_out, ffn_norm, w_gate_up, w_down, final_norm):
    B, L, D = x_prompt.shape
    DB = x_sample.shape[0]
    depth = w_in.shape[0]
    n_fh = fox_f_bias.shape[1]
    hd = FOX_HEAD_DIM
    pw = _prepare(attn_norm, w_in, fox_f_bias, gla_gate_w2, gla_gate_b, hgrn_lb_logits,
                  hgrn_out_norm, gla_out_norm, w_out, ffn_norm, w_gate_up, w_down, final_norm)
    tm = 512
    M = B * L
    xp = x_prompt
    xs = x_sample.reshape(1, DB, D)
    kp, vp, lfp, hsp, gsp = [], [], [], [], []
    ksm, vsm, lfs, hss, gss = [], [], [], [], []
    r2 = lambda a: a.reshape(M, a.shape[-1])
    s2 = lambda a: a.reshape(DB, a.shape[-1])
    for l in range(depth):
        final = l == depth - 1
        a = _project(xp, l, True, tm, pw)
        o_hg, s_hg = _gla_prompt(a["hq"], a["hk"], a["hlf"], a["hv"], HG_DK, HG_DV, "hgrn_prompt")
        o_fox = _fox_prompt(a["fq"], a["fk_b"], a["fvt_b"], a["c"], tm)
        o_gla, s_gla = _gla_prompt(a["gq"], a["gk"], a["glg"], a["gv"], GLA_DK, GLA_DV, "gla_prompt")
        x1, h2 = _merge(r2(xp), r2(o_hg), r2(a["hgate"]), r2(o_fox), r2(o_gla), r2(a["ggate"]),
                        l, tm, pw)
        xp = _ffn(h2, x1, l, final, tm, pw).reshape(B, L, D)
        kp.append(jnp.transpose(a["fkt"].reshape(B, n_fh, hd, L), (0, 3, 1, 2)))
        vp.append(jnp.transpose(a["fvt"].reshape(B, n_fh, hd, L), (0, 3, 1, 2)))
        lfp.append(a["flf"])
        hsp.append(s_hg)
        gsp.append(s_gla)
        a = _project(xs, l, False, DB, pw)
        o_hg, s_hg = _lin_step(state_hgrn[l], s2(a["hq"]), s2(a["hk"]), s2(a["hlf"]), s2(a["hv"]),
                               "hgrn_step")
        o_fox = _fox_decode(s2(a["fq"]).astype(F32), s2(a["fk"]), s2(a["fv"]), s2(a["flf"]),
                            cache_fox_k, cache_fox_v, cache_fox_logf, page_table, l)
        o_gla, s_gla = _lin_step(state_gla[l], s2(a["gq"]), s2(a["gk"]), s2(a["glg"]), s2(a["gv"]),
                                 "gla_step")
        x1, h2 = _merge(s2(xs), o_hg, s2(a["hgate"]), o_fox, o_gla, s2(a["ggate"]), l, DB, pw)
        xs = _ffn(h2, x1, l, final, DB, pw).reshape(1, DB, D)
        ksm.append(a["fk"].reshape(DB, 1, n_fh, hd))
        vsm.append(a["fv"].reshape(DB, 1, n_fh, hd))
        lfs.append(a["flf"].reshape(DB, 1, n_fh))
        hss.append(s_hg)
        gss.append(s_gla)
    return (xp, xs.reshape(DB, 1, D),
            jnp.stack(kp), jnp.stack(vp), jnp.stack(lfp), jnp.stack(hsp), jnp.stack(gsp),
            jnp.stack(ksm), jnp.stack(vsm), jnp.stack(lfs), jnp.stack(hss), jnp.stack(gss))
```

```python
import functools

import numpy as np
import jax
import jax.numpy as jnp
from jax import lax
from jax.experimental import pallas as pl
from jax.experimental.pallas import tpu as pltpu

F32 = jnp.float32
BF16 = jnp.bfloat16

HG_DK = 64
HG_DV = 64
FOX_HEAD_DIM = 64
GLA_DK = 32
GLA_DV = 64
GLA_GATE_RANK = 16
GLA_GATE_NORMALIZER = 16.0
RMS_EPS = 1e-6
LOG2E = 1.4426950408889634

LANES = 128
CHUNK = 64
SUB = 16
NSUB = CHUNK // SUB
GLA_SEQS_PER_STEP = 8
NEG_BIG = -1e30
VMEM_LIMIT = 48 * 1024 * 1024


def _cparams(sem):
    return pltpu.CompilerParams(dimension_semantics=sem, vmem_limit_bytes=VMEM_LIMIT)


def _shift(n):
    s = int(np.log2(n))
    assert (1 << s) == n
    return s


def _iota(shape, dim):
    return lax.broadcasted_iota(jnp.int32, shape, dim)


def _split3(x):
    x1 = x.astype(BF16)
    r1 = x - x1.astype(F32)
    x2 = r1.astype(BF16)
    r2 = r1 - x2.astype(F32)
    return x1, x2, r2.astype(BF16)


def _log_sigmoid(x):
    return jnp.minimum(x, 0.0) - jnp.log1p(jnp.exp(-jnp.abs(x)))


def _sigmoid(x):
    return 1.0 / (1.0 + jnp.exp(-x))


def _dot(a, b):
    return jnp.dot(a, b, preferred_element_type=F32)


def _dot_nt(a, b):
    return lax.dot_general(a, b, (((1,), (1,)), ((), ())), preferred_element_type=F32)


def _dot_tn(a, b):
    return lax.dot_general(a, b, (((0,), (0,)), ((), ())), preferred_element_type=F32)


def _rms(x, w):
    ms = jnp.mean(x * x, axis=-1, keepdims=True)
    return x * lax.rsqrt(ms + RMS_EPS) * w


W_PAD = 3456
MISC = 3328
FOX_W = 512
FOX_VROWS = FOX_HEAD_DIM + 16
COMMON_OUT = ("hq", "hk", "hlf", "hv", "hgate", "fq", "flf", "gq", "gk", "gv", "glg", "ggate")
PROMPT_OUT = COMMON_OUT + ("fk_b", "fkt", "fvt", "fvt_b", "c")
SAMPLE_OUT = COMMON_OUT + ("fk", "fv")


def _proj_kernel(layer, prompt, *refs):
    (x_ref, norm_ref, w_ref, fbias_ref, lbl_ref, w2_ref, gb_ref) = refs[:7]
    k = 7
    if prompt:
        wkvt_ref, tri_ref = refs[7], refs[8]
        k = 9
    names = PROMPT_OUT if prompt else SAMPLE_OUT
    o = dict(zip(names, refs[k:k + len(names)]))
    if prompt:
        carry_ref = refs[k + len(names)]

    hb = _rms(x_ref[0], norm_ref[...]).astype(BF16)

    def seg(a, b):
        return _dot(hb, w_ref[:, a:b])

    lbl = lbl_ref[...]
    e = jnp.exp(lbl - jnp.max(lbl, axis=0, keepdims=True))
    sm = e / jnp.sum(e, axis=0, keepdims=True)
    lb = jnp.zeros((1, lbl.shape[1]), F32)
    for i in range(1, layer + 1):
        lb = lb + sm[i:i + 1, :]
    o["hq"][0] = seg(0, 256)
    f = lb + (1.0 - lb) * _sigmoid(seg(256, 512))
    o["hlf"][0] = jnp.log(f)
    o["hk"][0] = 1.0 - f
    o["hv"][0] = seg(512, 768).astype(o["hv"].dtype)
    o["hgate"][0] = seg(768, 1024)
    q_scale = FOX_HEAD_DIM ** -0.5 * (LOG2E if prompt else 1.0)
    o["fq"][0] = (seg(1024, 1536) * q_scale).astype(BF16)
    if prompt:
        o["fk_b"][0] = seg(1536, 2048).astype(BF16)
        kvt = _dot_nt(wkvt_ref[...], hb)
        o["fkt"][0] = kvt[0:FOX_W]
        o["fvt"][0] = kvt[FOX_W:2 * FOX_W]
        hd = FOX_HEAD_DIM
        ones = jnp.ones((FOX_VROWS - hd, kvt.shape[1]), BF16)
        for h in range(FOX_W // hd):
            r = h * FOX_VROWS
            o["fvt_b"][0, r:r + hd, :] = kvt[FOX_W + h * hd:FOX_W + (h + 1) * hd].astype(BF16)
            o["fvt_b"][0, r + hd:r + FOX_VROWS, :] = ones
    else:
        o["fk"][0] = seg(1536, 2048)
        o["fv"][0] = seg(2048, 2560)
    o["gq"][0] = seg(2560, 2688) * (GLA_DK ** -0.5)
    o["gk"][0] = seg(2688, 2816)
    o["gv"][0] = seg(2816, 3072).astype(o["gv"].dtype)
    o["ggate"][0] = seg(3072, 3328)
    misc = seg(MISC, W_PAD)
    lf_all = _log_sigmoid(misc + fbias_ref[...])
    n_fh = o["flf"].shape[-1]
    o["flf"][0] = lf_all[:, 0:n_fh]
    g_logit = _dot(misc.astype(BF16), w2_ref[...]) + gb_ref[...]
    o["glg"][0] = _log_sigmoid(g_logit) / GLA_GATE_NORMALIZER

    if prompt:
        @pl.when(pl.program_id(1) == 0)
        def _():
            carry_ref[...] = jnp.zeros_like(carry_ref)

        x1, x2, x3 = _split3(lf_all)
        tri = tri_ref[...]
        c_all = _dot(tri, x1) + _dot(tri, x2) + _dot(tri, x3) + carry_ref[...]
        o["c"][0] = c_all[:, 0:n_fh] * LOG2E
        tm = c_all.shape[0]
        carry_ref[...] = c_all[tm - 1:tm, :]


def _project(x3, layer, prompt, tm, pw):
    B, L, D = x3.shape
    n_fh = pw["n_fox_heads"]
    grid = (B, L // tm)
    row = lambda w: pl.BlockSpec((1, tm, w), lambda b, t: (b, t, 0))
    const = lambda shp: pl.BlockSpec(shp, lambda b, t: (0,) * len(shp))
    v_dtype = BF16 if prompt else F32
    rows = dict(hq=(256, F32), hk=(256, F32), hlf=(256, F32), hv=(256, v_dtype), hgate=(256, F32),
                fq=(FOX_W, BF16), flf=(n_fh, F32), gq=(128, F32), gk=(128, F32), gv=(256, v_dtype),
                glg=(128, F32), ggate=(256, F32), fk_b=(FOX_W, BF16), c=(n_fh, F32),
                fk=(FOX_W, F32), fv=(FOX_W, F32))
    cols = dict(fkt=F32, fvt=F32, fvt_b=BF16)
    in_arrays = [x3, pw["attn_norm"][layer], pw["w_in"][layer], pw["fbias"][layer],
                 pw["lb_logits"], pw["w2pad"][layer], pw["gla_b"][layer]]
    in_specs = [row(D), const((1, D)), const((D, W_PAD)), const((1, LANES)),
                const(pw["lb_logits"].shape), const((LANES, LANES)), const((1, LANES))]
    scratch = []
    if prompt:
        tri = jnp.asarray(np.tril(np.ones((tm, tm), np.float32)), BF16)
        in_arrays += [pw["w_kvt"][layer], tri]
        in_specs += [const((2 * FOX_W, D)), const((tm, tm))]
        scratch.append(pltpu.VMEM((1, LANES), F32))
    names = PROMPT_OUT if prompt else SAMPLE_OUT
    out_shape, out_specs = [], []
    for n in names:
        if n in cols:
            nrow = FOX_W // FOX_HEAD_DIM * FOX_VROWS if n == "fvt_b" else FOX_W
            out_shape.append(jax.ShapeDtypeStruct((B, nrow, L), cols[n]))
            out_specs.append(pl.BlockSpec((1, nrow, tm), lambda b, t: (b, 0, t)))
        else:
            w, dt = rows[n]
            out_shape.append(jax.ShapeDtypeStruct((B, L, w), dt))
            out_specs.append(row(w))
    outs = pl.pallas_call(
        functools.partial(_proj_kernel, layer, prompt),
        grid=grid, in_specs=in_specs, out_specs=out_specs, out_shape=out_shape,
        scratch_shapes=scratch,
        compiler_params=_cparams(("parallel", "arbitrary")),
        name="proj_prompt" if prompt else "proj_sample",
    )(*in_arrays)
    return dict(zip(names, outs))


def _gla_chunk_kernel(dk, dv, n_chunks, q_ref, k_ref, g_ref, v_ref, tt_ref, o_ref, st_ref, s_scr):
    hdk = q_ref.shape[-1]
    hdv = v_ref.shape[-1]
    nh = hdk // dk
    C = CHUNK

    @pl.when(pl.program_id(1) == 0)
    def _():
        s_scr[...] = jnp.zeros_like(s_scr)

    row_k = _iota((C, hdk), 0)
    head_k = _iota((C, hdk), 1) >> _shift(dk)
    head_v = _iota((C, hdv), 1) >> _shift(dv)
    a_row = _iota((C, nh * C), 0)
    a_col = _iota((C, nh * C), 1) & (C - 1)
    causal = a_col <= a_row
    bd_mask = (_iota((hdv, hdk), 0) >> _shift(dv)) == (_iota((hdv, hdk), 1) >> _shift(dk))

    def stage_cumsum(bb, r0):
        g1, g2, g3 = _split3(g_ref[bb, pl.ds(r0, C), :])
        gcat = jnp.concatenate([g1, g2, g3], axis=0)
        return _dot(tt_ref[...], gcat)

    def stage_scores(bb, r0, bw):
        q = q_ref[bb, pl.ds(r0, C), :]
        k = k_ref[bb, pl.ds(r0, C), :]
        b = bw[0:C]
        w = bw[C:2 * C]
        qt = q * jnp.exp(w)
        qb = (q * jnp.exp(b)).astype(BF16)
        bend = b[C - 1:C, :]
        kdec = (k * jnp.exp(bend - b)).astype(BF16)
        lhs_parts = []
        kt_parts = []
        for i in range(NSUB):
            in_sub = (row_k >> _shift(SUB)) == i
            lhs_parts.append(jnp.where(in_sub, qt, 0.0).astype(BF16))
            if i == 0:
                arg = jnp.where(row_k < SUB, -b, NEG_BIG)
            else:
                arg = jnp.where(row_k < SUB * (i + 1), b[SUB * i - 1:SUB * i, :] - b, NEG_BIG)
            kt_parts.append(k * jnp.exp(arg))
        lhs = jnp.concatenate(lhs_parts, axis=1)
        rt_rows = []
        for h in range(nh):
            rt_rows.append(jnp.concatenate(
                [jnp.where(head_k == h, kt, 0.0).astype(BF16) for kt in kt_parts], axis=1))
        rt = jnp.concatenate(rt_rows, axis=0)
        a_all = _dot_nt(lhs, rt)
        return a_all, qb, kdec, bend

    def stage_output(bb, r0, a_all, qb, kdec, bend):
        v = v_ref[bb, pl.ds(r0, C), :].astype(BF16)
        a_m = jnp.where(causal, a_all, 0.0).astype(BF16)
        vbd = jnp.concatenate([jnp.where(head_v == h, v, 0).astype(BF16) for h in range(nh)], axis=0)
        s_t = s_scr[bb]
        o = _dot(a_m, vbd) + _dot_nt(qb, s_t.astype(BF16))
        upd = _dot_tn(v, kdec)
        return o, s_t * jnp.exp(bend) + jnp.where(bd_mask, upd, 0.0)

    def body(ci, carry):
        r0 = pl.multiple_of(ci * C, C)
        seqs = range(q_ref.shape[0])
        bws = [stage_cumsum(bb, r0) for bb in seqs]
        mids = [stage_scores(bb, r0, bws[bb]) for bb in seqs]
        res = [stage_output(bb, r0, *mids[bb]) for bb in seqs]
        o_ref[:, pl.ds(r0, C), :] = jnp.stack([r[0] for r in res], axis=0)
        s_scr[...] = jnp.stack([r[1] for r in res], axis=0)
        return carry

    lax.fori_loop(0, n_chunks, body, 0)

    @pl.when(pl.program_id(1) == pl.num_programs(1) - 1)
    def _():
        st_ref[...] = s_scr[...]


def _gla_prompt(q, k, g, v, dk, dv, name):
    B, L, hdk = q.shape
    hdv = v.shape[-1]
    nh = hdk // dk
    assert nh == hdv // dv and nh * CHUNK <= 256
    rows = 256 if L % 256 == 0 else L
    assert rows % CHUNK == 0 and L % rows == 0
    C = CHUNK
    t_full = np.tril(np.ones((C, C), np.float32))
    same_sub = (np.arange(C)[:, None] // SUB) == (np.arange(C)[None, :] // SUB)
    t_sub = t_full * same_sub
    tt = np.concatenate([np.tile(t_full, (1, 3)), np.tile(t_sub, (1, 3))], axis=0)
    tt = jnp.asarray(tt, BF16)
    nb = GLA_SEQS_PER_STEP if B % GLA_SEQS_PER_STEP == 0 else 1
    blk = lambda w: pl.BlockSpec((nb, rows, w), lambda b, t: (b, t, 0))
    o, st = pl.pallas_call(
        functools.partial(_gla_chunk_kernel, dk, dv, rows // C),
        grid=(B // nb, L // rows),
        in_specs=[blk(hdk), blk(hdk), blk(hdk), blk(hdv),
                  pl.BlockSpec((2 * C, 3 * C), lambda b, t: (0, 0))],
        out_specs=[blk(hdv), pl.BlockSpec((nb, hdv, hdk), lambda b, t: (b, 0, 0))],
        out_shape=[jax.ShapeDtypeStruct((B, L, hdv), F32),
                   jax.ShapeDtypeStruct((B, hdv, hdk), F32)],
        scratch_shapes=[pltpu.VMEM((nb, hdv, hdk), F32)],
        compiler_params=_cparams(("parallel", "arbitrary")),
        name=name,
    )(q, k, g, v, tt)
    st5 = st.reshape(B, nh, dv, nh, dk)
    idx = jnp.arange(nh)
    state = st5[:, idx, :, idx, :]
    return o, jnp.transpose(state, (1, 0, 3, 2))


FOX_HG = 4


def _fox_prompt_kernel(q_ref, k_ref, vt_ref, c_ref, o_ref, qm_scr, m_scr, acc_scr):
    i = pl.program_id(2)
    j = pl.program_id(3)
    tq = q_ref.shape[1]
    tk = k_ref.shape[1]
    w = q_ref.shape[2]
    hd = FOX_HEAD_DIM
    vr = FOX_VROWS

    @pl.when(j == 0)
    def _():
        q = q_ref[0]
        lane_head = _iota((tq, w), 1) >> _shift(hd)
        for h in range(FOX_HG):
            qm_scr[h] = jnp.where(lane_head == h, q, 0).astype(BF16)
        m_scr[...] = jnp.full_like(m_scr, -jnp.inf)
        acc_scr[...] = jnp.zeros_like(acc_scr)

    def compute(masked):
        k4 = k_ref[0]
        c = c_ref[0, 0]
        if masked:
            keep = _iota((tk, tq), 0) <= _iota((tk, tq), 1)
        def scores(h):
            s = _dot_nt(k4, qm_scr[h]) - c[:, h:h + 1]
            if masked:
                s = jnp.where(keep, s, -jnp.inf)
            return s, jnp.max(s, axis=0, keepdims=True)

        def softmax_pv(h, s, mx):
            m_prev = m_scr[h]
            m_new = jnp.maximum(m_prev, mx)
            alpha = jnp.exp2(m_prev - m_new)
            p = jnp.exp2(s - m_new)
            m_scr[h] = m_new
            pv = _dot(vt_ref[0, h * vr:(h + 1) * vr, :], p.astype(BF16))
            acc_scr[h] = acc_scr[h] * alpha + pv

        pending = [scores(h) for h in range(FOX_HG)]
        for h in range(FOX_HG):
            softmax_pv(h, *pending[h])

    @pl.when(j < i)
    def _():
        compute(False)

    @pl.when(j == i)
    def _():
        compute(True)
        ot = jnp.concatenate([acc_scr[h, 0:hd, :] / acc_scr[h, hd:hd + 1, :] for h in range(FOX_HG)],
                             axis=0)
        o_ref[0] = ot.T


def _fox_prompt(q, k, vt, c, tile):
    B, L, W = q.shape
    n_heads = c.shape[-1]
    hd = FOX_HEAD_DIM
    gw = FOX_HG * hd
    n_hg = W // gw
    nq = L // tile
    qspec = pl.BlockSpec((1, tile, gw), lambda b, g, i, j: (b, i, g))
    kspec = pl.BlockSpec((1, tile, gw), lambda b, g, i, j: (b, jnp.minimum(i, j), g))
    vspec = pl.BlockSpec((1, FOX_HG * FOX_VROWS, tile), lambda b, g, i, j: (b, g, jnp.minimum(i, j)))
    c4 = jnp.transpose(c.reshape(B, L, n_hg, FOX_HG), (0, 2, 1, 3))
    cspec = pl.BlockSpec((1, 1, tile, FOX_HG), lambda b, g, i, j: (b, g, jnp.minimum(i, j), 0))
    return pl.pallas_call(
        _fox_prompt_kernel,
        grid=(B, n_hg, nq, nq),
        in_specs=[qspec, kspec, vspec, cspec],
        out_specs=qspec,
        out_shape=jax.ShapeDtypeStruct((B, L, W), F32),
        scratch_shapes=[pltpu.VMEM((FOX_HG, tile, gw), BF16),
                        pltpu.VMEM((FOX_HG, 1, tile), F32),
                        pltpu.VMEM((FOX_HG, FOX_VROWS, tile), F32)],
        compiler_params=_cparams(("parallel", "parallel", "parallel", "arbitrary")),
        name="fox_prompt",
    )(q, k, vt, c4)


def _head_norm_gate(o, gate, w_row, gsum):
    y = o * o
    y1 = y.astype(BF16)
    y2 = (y - y1.astype(F32)).astype(BF16)
    ms = _dot(jnp.concatenate([y1, y2], axis=1), gsum)
    return o * lax.rsqrt(ms + RMS_EPS) * w_row * (gate * _sigmoid(gate))


def _merge_kernel(x_ref, ohg_ref, ghg_ref, ofox_ref, ogla_ref, ggla_ref, hgn_ref, glan_ref,
                  gsum_ref, wout_ref, ffn_norm_ref, x1_ref, h2_ref):
    gsum = gsum_ref[...]
    hg = _head_norm_gate(ohg_ref[...], ghg_ref[...], hgn_ref[...], gsum)
    gla = _head_norm_gate(ogla_ref[...], ggla_ref[...], glan_ref[...], gsum)
    mix = jnp.concatenate([hg.astype(BF16), ofox_ref[...].astype(BF16), gla.astype(BF16)], axis=1)
    x1 = x_ref[...] + _dot(mix, wout_ref[...])
    x1_ref[...] = x1
    h2_ref[...] = _rms(x1, ffn_norm_ref[...]).astype(BF16)


def _merge(x, ohg, ghg, ofox, ogla, ggla, layer, tm, pw):
    M, D = x.shape
    row = lambda w: pl.BlockSpec((tm, w), lambda t: (t, 0))
    const = lambda shp: pl.BlockSpec(shp, lambda t: (0,) * len(shp))
    return pl.pallas_call(
        _merge_kernel,
        grid=(M // tm,),
        in_specs=[row(D), row(256), row(256), row(512), row(256), row(256),
                  const((1, 256)), const((1, 256)), const((512, 256)), const((D, D)), const((1, D))],
        out_specs=[row(D), row(D)],
        out_shape=[jax.ShapeDtypeStruct((M, D), F32), jax.ShapeDtypeStruct((M, D), BF16)],
        compiler_params=_cparams(("parallel",)),
        name="merge",
    )(x, ohg, ghg, ofox, ogla, ggla, pw["hgn"][layer], pw["glan"][layer], pw["gsum"],
      pw["w_out"][layer], pw["ffn_norm"][layer])


def _ffn_kernel(final, h_ref, x1_ref, wg_ref, wu_ref, wd_ref, fn_ref, o_ref, acc_ref):
    j = pl.program_id(1)

    @pl.when(j == 0)
    def _():
        acc_ref[...] = x1_ref[...]

    h = h_ref[...]
    gate = _dot(h, wg_ref[...])
    up = _dot(h, wu_ref[...])
    act = (gate * _sigmoid(gate) * up).astype(BF16)
    acc_ref[...] += _dot(act, wd_ref[...])

    @pl.when(j == pl.num_programs(1) - 1)
    def _():
        x2 = acc_ref[...]
        o_ref[...] = _rms(x2, fn_ref[...]) if final else x2


def _ffn(h2, x1, layer, final, tm, pw):
    M, D = x1.shape
    ff = pw["w_down"][layer].shape[0]
    tf = pw["ff_tile"]
    nf = ff // tf
    return pl.pallas_call(
        functools.partial(_ffn_kernel, final),
        grid=(M // tm, nf),
        in_specs=[pl.BlockSpec((tm, D), lambda t, j: (t, 0)),
                  pl.BlockSpec((tm, D), lambda t, j: (t, 0)),
                  pl.BlockSpec((D, tf), lambda t, j: (0, j)),
                  pl.BlockSpec((D, tf), lambda t, j: (0, j + nf)),
                  pl.BlockSpec((tf, D), lambda t, j: (j, 0)),
                  pl.BlockSpec((1, D), lambda t, j: (0, 0))],
        out_specs=pl.BlockSpec((tm, D), lambda t, j: (t, 0)),
        out_shape=jax.ShapeDtypeStruct((M, D), F32),
        scratch_shapes=[pltpu.VMEM((tm, D), F32)],
        compiler_params=_cparams(("parallel", "arbitrary")),
        name="ffn",
    )(h2, x1, pw["w_gu"][layer], pw["w_gu"][layer], pw["w_down"][layer], pw["final_norm"])


def _lin_step_kernel(s_ref, q_ref, k_ref, g_ref, v_ref, o_ref, sn_ref):
    nh = s_ref.shape[1]
    for h in range(nh):
        s_new = jnp.exp(g_ref[0, h]) * s_ref[0, h] + k_ref[0, h] * v_ref[0, h]
        sn_ref[0, h] = s_new
        o_ref[0, h] = jnp.sum(q_ref[0, h] * s_new, axis=0, keepdims=True)


def _lin_step(state, q, k, g, v, name):
    B, nh, dk, dv = state.shape
    col = lambda a: a.reshape(B, nh, dk, 1)
    cspec = pl.BlockSpec((1, nh, dk, 1), lambda b: (b, 0, 0, 0))
    sspec = pl.BlockSpec((1, nh, dk, dv), lambda b: (b, 0, 0, 0))
    vspec = pl.BlockSpec((1, nh, 1, dv), lambda b: (b, 0, 0, 0))
    o, sn = pl.pallas_call(
        _lin_step_kernel,
        grid=(B,),
        in_specs=[sspec, cspec, cspec, cspec, vspec],
        out_specs=[vspec, sspec],
        out_shape=[jax.ShapeDtypeStruct((B, nh, 1, dv), F32),
                   jax.ShapeDtypeStruct((B, nh, dk, dv), F32)],
        compiler_params=_cparams(("parallel",)),
        name=name,
    )(state, col(q), col(k), col(g), v.reshape(B, nh, 1, dv))
    return o.reshape(B, nh * dv), sn


PAGES_PER_STEP = 8


def _fox_decode_kernel(npg, n_pages, pt_ref, *refs):
    qrow_ref, krow_ref, qcol_ref, vcol_ref, lfnew_ref, u_ref, lf_ref = refs[:7]
    k_refs = refs[7:7 + npg]
    v_refs = refs[7 + npg:7 + 2 * npg]
    o_ref = refs[7 + 2 * npg]
    qb_scr, m_scr, l_scr, t_scr, acc_scr, s_scr, p_scr = refs[8 + 2 * npg:]
    j = pl.program_id(1)
    pt0 = pl.program_id(0) * n_pages + (pl.num_programs(1) - 1 - j) * npg
    nh, hd, page = qb_scr.shape

    @pl.when(j == 0)
    def _():
        qb_scr[...] = jnp.broadcast_to(qcol_ref[0], (nh, hd, page))
        m_scr[...] = jnp.sum(qrow_ref[0] * krow_ref[0], axis=1, keepdims=True)
        l_scr[...] = jnp.ones_like(l_scr)
        t_scr[...] = lfnew_ref[0]
        acc_scr[...] = jnp.where(_iota((nh, hd, page), 2) == 0,
                                 jnp.broadcast_to(vcol_ref[0], (nh, hd, page)), 0.0)

    u = u_ref[...]
    t_run = t_scr[...]
    for i in reversed(range(npg)):
        x = lf_ref[pt_ref[pt0 + i]]
        x1, x2, x3 = _split3(x)
        cat = jnp.concatenate([x1.astype(F32), x2.astype(F32), x3.astype(F32),
                               jnp.zeros_like(x)], axis=0).astype(BF16)
        y4 = _dot(cat, u)
        y = y4[0:nh] + y4[nh:2 * nh] + y4[2 * nh:3 * nh]
        for h in range(nh):
            s_scr[i, h:h + 1, :] = jnp.sum(k_refs[i][h] * qb_scr[h], axis=0, keepdims=True)
        s_scr[i] = s_scr[i] + (y + t_run)
        t_run = t_run + (y[:, 0:1] + x[:, 0:1])
    t_scr[...] = t_run
    s_all = s_scr[...]
    m_prev = m_scr[...]
    m_new = jnp.maximum(m_prev, jnp.max(jnp.max(s_all, axis=0), axis=1, keepdims=True))
    alpha = jnp.exp(m_prev - m_new)
    p_all = jnp.exp(s_all - m_new)
    p_scr[...] = p_all
    l_scr[...] = alpha * l_scr[...] + jnp.sum(jnp.sum(p_all, axis=0), axis=1, keepdims=True)
    m_scr[...] = m_new
    for h in range(nh):
        contrib = jnp.zeros((hd, page), F32)
        for i in range(npg):
            contrib = contrib + p_scr[i, h:h + 1, :] * v_refs[i][h]
        acc_scr[h] = acc_scr[h] * alpha[h:h + 1, :] + contrib

    @pl.when(j == pl.num_programs(1) - 1)
    def _():
        l = l_scr[...]
        for h in range(nh):
            o_ref[0, h] = jnp.sum(acc_scr[h], axis=1, keepdims=True) / l[h:h + 1, :]


def _fox_decode(q, knew, vnew, lfnew, cache_k, cache_v, cache_lf, page_table, layer):
    B, W = q.shape
    depth, n_pool, page, nh, hd = cache_k.shape
    n_pages = page_table.shape[1]
    npg = PAGES_PER_STEP
    assert n_pages % npg == 0
    ng = n_pages // npg
    ck = jnp.transpose(cache_k, (0, 1, 3, 4, 2))
    cv = jnp.transpose(cache_v, (0, 1, 3, 4, 2))
    clf = jnp.transpose(cache_lf, (0, 1, 3, 2))
    u = jnp.asarray(np.tril(np.ones((page, page), np.float32), -1), BF16)
    pt_flat = page_table.reshape(-1)

    def page_map(nd):
        def make(i):
            def index_map(b, j, pt):
                return (layer, pt[b * n_pages + (ng - 1 - j) * npg + i]) + (0,) * nd
            return index_map
        return make

    per_b = lambda shp: pl.BlockSpec((1,) + shp, lambda b, j, pt: (b,) + (0,) * len(shp))
    in_specs = [per_b((nh, hd)), per_b((nh, hd)), per_b((nh, hd, 1)), per_b((nh, hd, 1)),
                per_b((nh, 1)), pl.BlockSpec((page, page), lambda b, j, pt: (0, 0)),
                pl.BlockSpec((None, n_pool, nh, page), lambda b, j, pt: (layer, 0, 0, 0),
                             pipeline_mode=pl.Buffered(1))]
    in_specs += [pl.BlockSpec((None, None, nh, hd, page), page_map(3)(i)) for i in range(npg)]
    in_specs += [pl.BlockSpec((None, None, nh, hd, page), page_map(3)(i)) for i in range(npg)]
    out = pl.pallas_call(
        functools.partial(_fox_decode_kernel, npg, n_pages),
        grid_spec=pltpu.PrefetchScalarGridSpec(
            num_scalar_prefetch=1,
            grid=(B, ng),
            in_specs=in_specs,
            out_specs=pl.BlockSpec((1, nh, hd, 1), lambda b, j, pt: (b, 0, 0, 0)),
            scratch_shapes=[pltpu.VMEM((nh, hd, page), F32), pltpu.VMEM((nh, 1), F32),
                            pltpu.VMEM((nh, 1), F32), pltpu.VMEM((nh, 1), F32),
                            pltpu.VMEM((nh, hd, page), F32),
                            pltpu.VMEM((npg, nh, page), F32), pltpu.VMEM((npg, nh, page), F32)]),
        out_shape=jax.ShapeDtypeStruct((B, nh, hd, 1), F32),
        compiler_params=_cparams(("parallel", "arbitrary")),
        name="fox_decode",
    )(pt_flat, q.reshape(B, nh, hd), knew.reshape(B, nh, hd), q.reshape(B, nh, hd, 1),
      vnew.reshape(B, nh, hd, 1), lfnew.reshape(B, nh, 1), u,
      clf, *([ck] * npg), *([cv] * npg))
    return out.reshape(B, W)


def _prepare(attn_norm, w_in, fox_f_bias, gla_gate_w2, gla_gate_b, hgrn_lb_logits, hgrn_out_norm,
             gla_out_norm, w_out, ffn_norm, w_gate_up, w_down, final_norm):
    depth, D, in_width = w_in.shape
    n_fh = fox_f_bias.shape[1]
    main = 3328
    off_ff = 2560
    off_glr = in_width - GLA_GATE_RANK
    assert off_glr == off_ff + n_fh + (main - off_ff)
    w_main = jnp.concatenate([w_in[:, :, :off_ff], w_in[:, :, off_ff + n_fh:off_glr]], axis=2)
    misc = jnp.concatenate([w_in[:, :, off_ff:off_ff + n_fh], w_in[:, :, off_glr:],
                            jnp.zeros((depth, D, LANES - n_fh - GLA_GATE_RANK), w_in.dtype)], axis=2)
    w_p = jnp.concatenate([w_main, misc], axis=2).astype(BF16)
    assert w_p.shape[2] == W_PAD
    w_kvt = jnp.transpose(w_in[:, :, 1536:1536 + 2 * FOX_W], (0, 2, 1)).astype(BF16)
    fbias = jnp.zeros((depth, 1, LANES), F32).at[:, 0, :n_fh].set(fox_f_bias)
    w2pad = jnp.zeros((depth, LANES, LANES), F32).at[:, n_fh:n_fh + GLA_GATE_RANK, :].set(gla_gate_w2)
    ff = w_down.shape[1]
    tf = ff // 2 if (ff // 2) % LANES == 0 else ff
    n_hg_heads = 256 // HG_DV
    gsum = np.kron(np.eye(n_hg_heads, dtype=np.float32), np.full((HG_DV, HG_DV), 1.0 / HG_DV, np.float32))
    return dict(
        n_fox_heads=n_fh,
        attn_norm=attn_norm.reshape(depth, 1, D),
        w_in=w_p,
        w_kvt=w_kvt,
        fbias=fbias,
        lb_logits=hgrn_lb_logits,
        w2pad=w2pad.astype(BF16),
        gla_b=gla_gate_b.reshape(depth, 1, LANES),
        hgn=jnp.tile(hgrn_out_norm, (1, n_hg_heads)).reshape(depth, 1, 256),
        glan=jnp.tile(gla_out_norm, (1, 256 // GLA_DV)).reshape(depth, 1, 256),
        gsum=jnp.asarray(np.concatenate([gsum, gsum], axis=0), BF16),
        w_out=w_out.astype(BF16),
        ffn_norm=ffn_norm.reshape(depth, 1, D),
        w_gu=w_gate_up.astype(BF16),
        w_down=w_down.astype(BF16),
        ff_tile=tf,
        final_norm=final_norm.reshape(1, D),
    )


def kernel(x_prompt, x_sample, cache_fox_k, cache_fox_v, cache_fox_logf, state_hgrn, state_gla, page_table, attn_norm, w_in, fox_f_bias, gla_gate_w2, gla_gate_b, hgrn_lb_logits, hgrn_out_norm, gla_out_norm, w_out, ffn_norm, w_gate_up, w_down, final_norm):
    B, L, D = x_prompt.shape
    DB = x_sample.shape[0]
    depth = w_in.shape[0]
    n_fh = fox_f_bias.shape[1]
    hd = FOX_HEAD_DIM
    pw = _prepare(attn_norm, w_in, fox_f_bias, gla_gate_w2, gla_gate_b, hgrn_lb_logits,
                  hgrn_out_norm, gla_out_norm, w_out, ffn_norm, w_gate_up, w_down, final_norm)
    tm = 512
    M = B * L
    xp = x_prompt
    xs = x_sample.reshape(1, DB, D)
    kp, vp, lfp, hsp, gsp = [], [], [], [], []
    ksm, vsm, lfs, hss, gss = [], [], [], [], []
    r2 = lambda a: a.reshape(M, a.shape[-1])
    s2 = lambda a: a.reshape(DB, a.shape[-1])
    for l in range(depth):
        final = l == depth - 1
        a = _project(xp, l, True, tm, pw)
        o_hg, s_hg = _gla_prompt(a["hq"], a["hk"], a["hlf"], a["hv"], HG_DK, HG_DV, "hgrn_prompt")
        o_fox = _fox_prompt(a["fq"], a["fk_b"], a["fvt_b"], a["c"], tm)
        o_gla, s_gla = _gla_prompt(a["gq"], a["gk"], a["glg"], a["gv"], GLA_DK, GLA_DV, "gla_prompt")
        x1, h2 = _merge(r2(xp), r2(o_hg), r2(a["hgate"]), r2(o_fox), r2(o_gla), r2(a["ggate"]),
                        l, tm, pw)
        xp = _ffn(h2, x1, l, final, tm, pw).reshape(B, L, D)
        kp.append(jnp.transpose(a["fkt"].reshape(B, n_fh, hd, L), (0, 3, 1, 2)))
        vp.append(jnp.transpose(a["fvt"].reshape(B, n_fh, hd, L), (0, 3, 1, 2)))
        lfp.append(a["flf"])
        hsp.append(s_hg)
        gsp.append(s_gla)
        a = _project(xs, l, False, DB, pw)
        o_hg, s_hg = _lin_step(state_hgrn[l], s2(a["hq"]), s2(a["hk"]), s2(a["hlf"]), s2(a["hv"]),
                               "hgrn_step")
        o_fox = _fox_decode(s2(a["fq"]).astype(F32), s2(a["fk"]), s2(a["fv"]), s2(a["flf"]),
                            cache_fox_k, cache_fox_v, cache_fox_logf, page_table, l)
        o_gla, s_gla = _lin_step(state_gla[l], s2(a["gq"]), s2(a["gk"]), s2(a["glg"]), s2(a["gv"]),
                                 "gla_step")
        x1, h2 = _merge(s2(xs), o_hg, s2(a["hgate"]), o_fox, o_gla, s2(a["ggate"]), l, DB, pw)
        xs = _ffn(h2, x1, l, final, DB, pw).reshape(1, DB, D)
        ksm.append(a["fk"].reshape(DB, 1, n_fh, hd))
        vsm.append(a["fv"].reshape(DB, 1, n_fh, hd))
        lfs.append(a["flf"].reshape(DB, 1, n_fh))
        hss.append(s_hg)
        gss.append(s_gla)
    return (xp, xs.reshape(DB, 1, D),
            jnp.stack(kp), jnp.stack(vp), jnp.stack(lfp), jnp.stack(hsp), jnp.stack(gsp),
            jnp.stack(ksm), jnp.stack(vsm), jnp.stack(lfs), jnp.stack(hss), jnp.stack(gss))
```

```python
import functools

import numpy as np
import jax
import jax.numpy as jnp
from jax import lax
from jax.experimental import pallas as pl
from jax.experimental.pallas import tpu as pltpu

F32 = jnp.float32
BF16 = jnp.bfloat16

HG_DK = 64
HG_DV = 64
FOX_HEAD_DIM = 64
GLA_DK = 32
GLA_DV = 64
GLA_GATE_RANK = 16
GLA_GATE_NORMALIZER = 16.0
RMS_EPS = 1e-6
LOG2E = 1.4426950408889634

LANES = 128
CHUNK = 64
SUB = 16
NSUB = CHUNK // SUB
GLA_SEQS_PER_STEP = 8
NEG_BIG = -1e30
VMEM_LIMIT = 48 * 1024 * 1024


def _cparams(sem):
    return pltpu.CompilerParams(dimension_semantics=sem, vmem_limit_bytes=VMEM_LIMIT)


def _shift(n):
    s = int(np.log2(n))
    assert (1 << s) == n
    return s


def _iota(shape, dim):
    return lax.broadcasted_iota(jnp.int32, shape, dim)


def _split3(x):
    x1 = x.astype(BF16)
    r1 = x - x1.astype(F32)
    x2 = r1.astype(BF16)
    r2 = r1 - x2.astype(F32)
    return x1, x2, r2.astype(BF16)


def _log_sigmoid(x):
    return jnp.minimum(x, 0.0) - jnp.log1p(jnp.exp(-jnp.abs(x)))


def _sigmoid(x):
    return 1.0 / (1.0 + jnp.exp(-x))


def _dot(a, b):
    return jnp.dot(a, b, preferred_element_type=F32)


def _dot_nt(a, b):
    return lax.dot_general(a, b, (((1,), (1,)), ((), ())), preferred_element_type=F32)


def _dot_tn(a, b):
    return lax.dot_general(a, b, (((0,), (0,)), ((), ())), preferred_element_type=F32)


def _rms(x, w):
    ms = jnp.mean(x * x, axis=-1, keepdims=True)
    return x * lax.rsqrt(ms + RMS_EPS) * w


W_PAD = 3456
MISC = 3328
FOX_W = 512
FOX_VROWS = FOX_HEAD_DIM + 16
COMMON_OUT = ("hq", "hk", "hlf", "hv", "hgate", "fq", "flf", "gq", "gk", "gv", "glg", "ggate")
PROMPT_OUT = COMMON_OUT + ("fk_b", "fkt", "fvt", "fvt_b", "c")
SAMPLE_OUT = COMMON_OUT + ("fk", "fv")


def _proj_kernel(layer, prompt, *refs):
    (x_ref, norm_ref, w_ref, fbias_ref, lbl_ref, w2_ref, gb_ref) = refs[:7]
    k = 7
    if prompt:
        wkvt_ref, tri_ref = refs[7], refs[8]
        k = 9
    names = PROMPT_OUT if prompt else SAMPLE_OUT
    o = dict(zip(names, refs[k:k + len(names)]))
    if prompt:
        carry_ref = refs[k + len(names)]

    hb = _rms(x_ref[0], norm_ref[...]).astype(BF16)

    def seg(a, b):
        return _dot(hb, w_ref[:, a:b])

    lbl = lbl_ref[...]
    e = jnp.exp(lbl - jnp.max(lbl, axis=0, keepdims=True))
    sm = e / jnp.sum(e, axis=0, keepdims=True)
    lb = jnp.zeros((1, lbl.shape[1]), F32)
    for i in range(1, layer + 1):
        lb = lb + sm[i:i + 1, :]
    o["hq"][0] = seg(0, 256)
    f = lb + (1.0 - lb) * _sigmoid(seg(256, 512))
    o["hlf"][0] = jnp.log(f)
    o["hk"][0] = 1.0 - f
    o["hv"][0] = seg(512, 768).astype(o["hv"].dtype)
    o["hgate"][0] = seg(768, 1024)
    q_scale = FOX_HEAD_DIM ** -0.5 * (LOG2E if prompt else 1.0)
    o["fq"][0] = (seg(1024, 1536) * q_scale).astype(BF16)
    if prompt:
        kvt = _dot_nt(wkvt_ref[...], hb)
        o["fk_b"][0] = kvt[0:FOX_W].T.astype(BF16)
        o["fkt"][0] = kvt[0:FOX_W]
        o["fvt"][0] = kvt[FOX_W:2 * FOX_W]
        hd = FOX_HEAD_DIM
        ones = jnp.ones((FOX_VROWS - hd, kvt.shape[1]), BF16)
        for h in range(FOX_W // hd):
            r = h * FOX_VROWS
            o["fvt_b"][0, r:r + hd, :] = kvt[FOX_W + h * hd:FOX_W + (h + 1) * hd].astype(BF16)
            o["fvt_b"][0, r + hd:r + FOX_VROWS, :] = ones
    else:
        o["fk"][0] = seg(1536, 2048)
        o["fv"][0] = seg(2048, 2560)
    o["gq"][0] = seg(2560, 2688) * (GLA_DK ** -0.5)
    o["gk"][0] = seg(2688, 2816)
    o["gv"][0] = seg(2816, 3072).astype(o["gv"].dtype)
    o["ggate"][0] = seg(3072, 3328)
    misc = seg(MISC, W_PAD)
    lf_all = _log_sigmoid(misc + fbias_ref[...])
    n_fh = o["flf"].shape[-1]
    o["flf"][0] = lf_all[:, 0:n_fh]
    g_logit = _dot(misc.astype(BF16), w2_ref[...]) + gb_ref[...]
    o["glg"][0] = _log_sigmoid(g_logit) / GLA_GATE_NORMALIZER

    if prompt:
        @pl.when(pl.program_id(1) == 0)
        def _():
            carry_ref[...] = jnp.zeros_like(carry_ref)

        x1, x2, x3 = _split3(lf_all)
        tri = tri_ref[...]
        c_all = _dot(tri, x1) + _dot(tri, x2) + _dot(tri, x3) + carry_ref[...]
        o["c"][0] = c_all[:, 0:n_fh] * LOG2E
        tm = c_all.shape[0]
        carry_ref[...] = c_all[tm - 1:tm, :]


def _project(x3, layer, prompt, tm, pw):
    B, L, D = x3.shape
    n_fh = pw["n_fox_heads"]
    grid = (B, L // tm)
    row = lambda w: pl.BlockSpec((1, tm, w), lambda b, t: (b, t, 0))
    const = lambda shp: pl.BlockSpec(shp, lambda b, t: (0,) * len(shp))
    v_dtype = BF16 if prompt else F32
    rows = dict(hq=(256, F32), hk=(256, F32), hlf=(256, F32), hv=(256, v_dtype), hgate=(256, F32),
                fq=(FOX_W, BF16), flf=(n_fh, F32), gq=(128, F32), gk=(128, F32), gv=(256, v_dtype),
                glg=(128, F32), ggate=(256, F32), fk_b=(FOX_W, BF16), c=(n_fh, F32),
                fk=(FOX_W, F32), fv=(FOX_W, F32))
    cols = dict(fkt=F32, fvt=F32, fvt_b=BF16)
    in_arrays = [x3, pw["attn_norm"][layer], pw["w_in"][layer], pw["fbias"][layer],
                 pw["lb_logits"], pw["w2pad"][layer], pw["gla_b"][layer]]
    in_specs = [row(D), const((1, D)), const((D, W_PAD)), const((1, LANES)),
                const(pw["lb_logits"].shape), const((LANES, LANES)), const((1, LANES))]
    scratch = []
    if prompt:
        tri = jnp.asarray(np.tril(np.ones((tm, tm), np.float32)), BF16)
        in_arrays += [pw["w_kvt"][layer], tri]
        in_specs += [const((2 * FOX_W, D)), const((tm, tm))]
        scratch.append(pltpu.VMEM((1, LANES), F32))
    names = PROMPT_OUT if prompt else SAMPLE_OUT
    out_shape, out_specs = [], []
    for n in names:
        if n in cols:
            nrow = FOX_W // FOX_HEAD_DIM * FOX_VROWS if n == "fvt_b" else FOX_W
            out_shape.append(jax.ShapeDtypeStruct((B, nrow, L), cols[n]))
            out_specs.append(pl.BlockSpec((1, nrow, tm), lambda b, t: (b, 0, t)))
        else:
            w, dt = rows[n]
            out_shape.append(jax.ShapeDtypeStruct((B, L, w), dt))
            out_specs.append(row(w))
    outs = pl.pallas_call(
        functools.partial(_proj_kernel, layer, prompt),
        grid=grid, in_specs=in_specs, out_specs=out_specs, out_shape=out_shape,
        scratch_shapes=scratch,
        compiler_params=_cparams(("parallel", "arbitrary")),
        name="proj_prompt" if prompt else "proj_sample",
    )(*in_arrays)
    return dict(zip(names, outs))


def _gla_chunk_kernel(dk, dv, n_chunks, q_ref, k_ref, g_ref, v_ref, tt_ref, o_ref, st_ref, s_scr):
    hdk = q_ref.shape[-1]
    hdv = v_ref.shape[-1]
    nh = hdk // dk
    C = CHUNK

    @pl.when(pl.program_id(1) == 0)
    def _():
        s_scr[...] = jnp.zeros_like(s_scr)

    row_k = _iota((C, hdk), 0)
    head_k = _iota((C, hdk), 1) >> _shift(dk)
    head_v = _iota((C, hdv), 1) >> _shift(dv)
    a_row = _iota((C, nh * C), 0)
    a_col = _iota((C, nh * C), 1) & (C - 1)
    causal = a_col <= a_row
    bd_mask = (_iota((hdv, hdk), 0) >> _shift(dv)) == (_iota((hdv, hdk), 1) >> _shift(dk))

    def stage_cumsum(bb, r0):
        g1, g2, g3 = _split3(g_ref[bb, pl.ds(r0, C), :])
        gcat = jnp.concatenate([g1, g2, g3], axis=0)
        return _dot(tt_ref[...], gcat)

    def stage_scores(bb, r0, bw):
        q = q_ref[bb, pl.ds(r0, C), :]
        k = k_ref[bb, pl.ds(r0, C), :]
        b = bw[0:C]
        w = bw[C:2 * C]
        qt = q * jnp.exp(w)
        qb = (q * jnp.exp(b)).astype(BF16)
        bend = b[C - 1:C, :]
        kdec = (k * jnp.exp(bend - b)).astype(BF16)
        lhs_parts = []
        kt_parts = []
        for i in range(NSUB):
            in_sub = (row_k >> _shift(SUB)) == i
            lhs_parts.append(jnp.where(in_sub, qt, 0.0).astype(BF16))
            if i == 0:
                arg = jnp.where(row_k < SUB, -b, NEG_BIG)
            else:
                arg = jnp.where(row_k < SUB * (i + 1), b[SUB * i - 1:SUB * i, :] - b, NEG_BIG)
            kt_parts.append(k * jnp.exp(arg))
        lhs = jnp.concatenate(lhs_parts, axis=1)
        rt_rows = []
        for h in range(nh):
            rt_rows.append(jnp.concatenate(
                [jnp.where(head_k == h, kt, 0.0).astype(BF16) for kt in kt_parts], axis=1))
        rt = jnp.concatenate(rt_rows, axis=0)
        a_all = _dot_nt(lhs, rt)
        return a_all, qb, kdec, bend

    def stage_output(bb, r0, a_all, qb, kdec, bend):
        v = v_ref[bb, pl.ds(r0, C), :].astype(BF16)
        a_m = jnp.where(causal, a_all, 0.0).astype(BF16)
        vbd = jnp.concatenate([jnp.where(head_v == h, v, 0).astype(BF16) for h in range(nh)], axis=0)
        s_t = s_scr[bb]
        o = _dot(a_m, vbd) + _dot_nt(qb, s_t.astype(BF16))
        upd = _dot_tn(v, kdec)
        return o, s_t * jnp.exp(bend) + jnp.where(bd_mask, upd, 0.0)

    def body(ci, carry):
        r0 = pl.multiple_of(ci * C, C)
        seqs = range(q_ref.shape[0])
        bws = [stage_cumsum(bb, r0) for bb in seqs]
        mids = [stage_scores(bb, r0, bws[bb]) for bb in seqs]
        res = [stage_output(bb, r0, *mids[bb]) for bb in seqs]
        o_ref[:, pl.ds(r0, C), :] = jnp.stack([r[0] for r in res], axis=0)
        s_scr[...] = jnp.stack([r[1] for r in res], axis=0)
        return carry

    lax.fori_loop(0, n_chunks, body, 0)

    @pl.when(pl.program_id(1) == pl.num_programs(1) - 1)
    def _():
        st_ref[...] = s_scr[...]


def _gla_prompt(q, k, g, v, dk, dv, name):
    B, L, hdk = q.shape
    hdv = v.shape[-1]
    nh = hdk // dk
    assert nh == hdv // dv and nh * CHUNK <= 256
    rows = 256 if L % 256 == 0 else L
    assert rows % CHUNK == 0 and L % rows == 0
    C = CHUNK
    t_full = np.tril(np.ones((C, C), np.float32))
    same_sub = (np.arange(C)[:, None] // SUB) == (np.arange(C)[None, :] // SUB)
    t_sub = t_full * same_sub
    tt = np.concatenate([np.tile(t_full, (1, 3)), np.tile(t_sub, (1, 3))], axis=0)
    tt = jnp.asarray(tt, BF16)
    nb = GLA_SEQS_PER_STEP if B % GLA_SEQS_PER_STEP == 0 else 1
    blk = lambda w: pl.BlockSpec((nb, rows, w), lambda b, t: (b, t, 0))
    o, st = pl.pallas_call(
        functools.partial(_gla_chunk_kernel, dk, dv, rows // C),
        grid=(B // nb, L // rows),
        in_specs=[blk(hdk), blk(hdk), blk(hdk), blk(hdv),
                  pl.BlockSpec((2 * C, 3 * C), lambda b, t: (0, 0))],
        out_specs=[blk(hdv), pl.BlockSpec((nb, hdv, hdk), lambda b, t: (b, 0, 0))],
        out_shape=[jax.ShapeDtypeStruct((B, L, hdv), F32),
                   jax.ShapeDtypeStruct((B, hdv, hdk), F32)],
        scratch_shapes=[pltpu.VMEM((nb, hdv, hdk), F32)],
        compiler_params=_cparams(("parallel", "arbitrary")),
        name=name,
    )(q, k, g, v, tt)
    st5 = st.reshape(B, nh, dv, nh, dk)
    idx = jnp.arange(nh)
    state = st5[:, idx, :, idx, :]
    return o, jnp.transpose(state, (1, 0, 3, 2))


FOX_HG = 4


def _fox_prompt_kernel(qi_ref, kj_ref, q_ref, k_ref, vt_ref, c_ref, o_ref, qm_scr, m_scr, acc_scr):
    i = qi_ref[pl.program_id(2)]
    j = kj_ref[pl.program_id(2)]
    tq = q_ref.shape[1]
    tk = k_ref.shape[1]
    w = q_ref.shape[2]
    hd = FOX_HEAD_DIM
    vr = FOX_VROWS

    @pl.when(j == 0)
    def _():
        q = q_ref[0]
        lane_head = _iota((tq, w), 1) >> _shift(hd)
        for h in range(FOX_HG):
            qm_scr[h] = jnp.where(lane_head == h, q, 0).astype(BF16)
        m_scr[...] = jnp.full_like(m_scr, -jnp.inf)
        acc_scr[...] = jnp.zeros_like(acc_scr)

    def compute(masked):
        k4 = k_ref[0]
        c = c_ref[0, 0]
        if masked:
            keep = _iota((tk, tq), 0) <= _iota((tk, tq), 1)
        def scores(h):
            s = _dot_nt(k4, qm_scr[h]) - c[:, h:h + 1]
            if masked:
                s = jnp.where(keep, s, -jnp.inf)
            return s, jnp.max(s, axis=0, keepdims=True)

        def softmax_pv(h, s, mx):
            m_prev = m_scr[h]
            m_new = jnp.maximum(m_prev, mx)
            alpha = jnp.exp2(m_prev - m_new)
            p = jnp.exp2(s - m_new)
            m_scr[h] = m_new
            pv = _dot(vt_ref[0, h * vr:(h + 1) * vr, :], p.astype(BF16))
            acc_scr[h] = acc_scr[h] * alpha + pv

        pending = [scores(h) for h in range(FOX_HG)]
        for h in range(FOX_HG):
            softmax_pv(h, *pending[h])

    @pl.when(j < i)
    def _():
        compute(False)

    @pl.when(j == i)
    def _():
        compute(True)
        ot = jnp.concatenate([acc_scr[h, 0:hd, :] / acc_scr[h, hd:hd + 1, :] for h in range(FOX_HG)],
                             axis=0)
        o_ref[0] = ot.T


def _fox_prompt(q, k, vt, c, tile):
    B, L, W = q.shape
    n_heads = c.shape[-1]
    hd = FOX_HEAD_DIM
    gw = FOX_HG * hd
    n_hg = W // gw
    nq = L // tile
    pairs = [(i, j) for i in range(nq) for j in range(i + 1)]
    qi = jnp.asarray([p[0] for p in pairs], jnp.int32)
    kj = jnp.asarray([p[1] for p in pairs], jnp.int32)
    qspec = pl.BlockSpec((1, tile, gw), lambda b, g, s, qi, kj: (b, qi[s], g))
    kspec = pl.BlockSpec((1, tile, gw), lambda b, g, s, qi, kj: (b, kj[s], g))
    vspec = pl.BlockSpec((1, FOX_HG * FOX_VROWS, tile), lambda b, g, s, qi, kj: (b, g, kj[s]))
    c4 = jnp.transpose(c.reshape(B, L, n_hg, FOX_HG), (0, 2, 1, 3))
    cspec = pl.BlockSpec((1, 1, tile, FOX_HG), lambda b, g, s, qi, kj: (b, g, kj[s], 0))
    return pl.pallas_call(
        _fox_prompt_kernel,
        grid_spec=pltpu.PrefetchScalarGridSpec(
            num_scalar_prefetch=2,
            grid=(B, n_hg, len(pairs)),
            in_specs=[qspec, kspec, vspec, cspec],
            out_specs=qspec,
            scratch_shapes=[pltpu.VMEM((FOX_HG, tile, gw), BF16),
                            pltpu.VMEM((FOX_HG, 1, tile), F32),
                            pltpu.VMEM((FOX_HG, FOX_VROWS, tile), F32)]),
        out_shape=jax.ShapeDtypeStruct((B, L, W), F32),
        compiler_params=_cparams(("parallel", "parallel", "arbitrary")),
        name="fox_prompt",
    )(qi, kj, q, k, vt, c4)


def _head_norm_gate(o, gate, w_row, gsum):
    y = o * o
    y1 = y.astype(BF16)
    y2 = (y - y1.astype(F32)).astype(BF16)
    ms = _dot(jnp.concatenate([y1, y2], axis=1), gsum)
    return o * lax.rsqrt(ms + RMS_EPS) * w_row * (gate * _sigmoid(gate))


def _merge_kernel(x_ref, ohg_ref, ghg_ref, ofox_ref, ogla_ref, ggla_ref, hgn_ref, glan_ref,
                  gsum_ref, wout_ref, ffn_norm_ref, x1_ref, h2_ref):
    gsum = gsum_ref[...]
    hg = _head_norm_gate(ohg_ref[...], ghg_ref[...], hgn_ref[...], gsum)
    gla = _head_norm_gate(ogla_ref[...], ggla_ref[...], glan_ref[...], gsum)
    mix = jnp.concatenate([hg.astype(BF16), ofox_ref[...].astype(BF16), gla.astype(BF16)], axis=1)
    x1 = x_ref[...] + _dot(mix, wout_ref[...])
    x1_ref[...] = x1
    h2_ref[...] = _rms(x1, ffn_norm_ref[...]).astype(BF16)


def _merge(x, ohg, ghg, ofox, ogla, ggla, layer, tm, pw):
    M, D = x.shape
    row = lambda w: pl.BlockSpec((tm, w), lambda t: (t, 0))
    const = lambda shp: pl.BlockSpec(shp, lambda t: (0,) * len(shp))
    return pl.pallas_call(
        _merge_kernel,
        grid=(M // tm,),
        in_specs=[row(D), row(256), row(256), row(512), row(256), row(256),
                  const((1, 256)), const((1, 256)), const((512, 256)), const((D, D)), const((1, D))],
        out_specs=[row(D), row(D)],
        out_shape=[jax.ShapeDtypeStruct((M, D), F32), jax.ShapeDtypeStruct((M, D), BF16)],
        compiler_params=_cparams(("parallel",)),
        name="merge",
    )(x, ohg, ghg, ofox, ogla, ggla, pw["hgn"][layer], pw["glan"][layer], pw["gsum"],
      pw["w_out"][layer], pw["ffn_norm"][layer])


def _ffn_kernel(final, h_ref, x1_ref, wg_ref, wu_ref, wd_ref, fn_ref, o_ref, acc_ref):
    j = pl.program_id(1)

    @pl.when(j == 0)
    def _():
        acc_ref[...] = x1_ref[...]

    h = h_ref[...]
    gate = _dot(h, wg_ref[...])
    up = _dot(h, wu_ref[...])
    act = (gate * _sigmoid(gate) * up).astype(BF16)
    acc_ref[...] += _dot(act, wd_ref[...])

    @pl.when(j == pl.num_programs(1) - 1)
    def _():
        x2 = acc_ref[...]
        o_ref[...] = _rms(x2, fn_ref[...]) if final else x2


def _ffn(h2, x1, layer, final, tm, pw):
    M, D = x1.shape
    ff = pw["w_down"][layer].shape[0]
    tf = pw["ff_tile"]
    nf = ff // tf
    return pl.pallas_call(
        functools.partial(_ffn_kernel, final),
        grid=(M // tm, nf),
        in_specs=[pl.BlockSpec((tm, D), lambda t, j: (t, 0)),
                  pl.BlockSpec((tm, D), lambda t, j: (t, 0)),
                  pl.BlockSpec((D, tf), lambda t, j: (0, j)),
                  pl.BlockSpec((D, tf), lambda t, j: (0, j + nf)),
                  pl.BlockSpec((tf, D), lambda t, j: (j, 0)),
                  pl.BlockSpec((1, D), lambda t, j: (0, 0))],
        out_specs=pl.BlockSpec((tm, D), lambda t, j: (t, 0)),
        out_shape=jax.ShapeDtypeStruct((M, D), F32),
        scratch_shapes=[pltpu.VMEM((tm, D), F32)],
        compiler_params=_cparams(("parallel", "arbitrary")),
        name="ffn",
    )(h2, x1, pw["w_gu"][layer], pw["w_gu"][layer], pw["w_down"][layer], pw["final_norm"])


LIN_STEP_SEQS = 8


def _lin_step_kernel(s_ref, q_ref, k_ref, g_ref, v_ref, o_ref, sn_ref):
    nb, nh = s_ref.shape[:2]
    for b in range(nb):
        for h in range(nh):
            s_new = jnp.exp(g_ref[b, h]) * s_ref[b, h] + k_ref[b, h] * v_ref[b, h]
            sn_ref[b, h] = s_new
            o_ref[b, h] = jnp.sum(q_ref[b, h] * s_new, axis=0, keepdims=True)


def _lin_step(state, q, k, g, v, name):
    B, nh, dk, dv = state.shape
    col = lambda a: a.reshape(B, nh, dk, 1)
    nb = LIN_STEP_SEQS if B % LIN_STEP_SEQS == 0 else 1
    cspec = pl.BlockSpec((nb, nh, dk, 1), lambda b: (b, 0, 0, 0))
    sspec = pl.BlockSpec((nb, nh, dk, dv), lambda b: (b, 0, 0, 0))
    vspec = pl.BlockSpec((nb, nh, 1, dv), lambda b: (b, 0, 0, 0))
    o, sn = pl.pallas_call(
        _lin_step_kernel,
        grid=(B // nb,),
        in_specs=[sspec, cspec, cspec, cspec, vspec],
        out_specs=[vspec, sspec],
        out_shape=[jax.ShapeDtypeStruct((B, nh, 1, dv), F32),
                   jax.ShapeDtypeStruct((B, nh, dk, dv), F32)],
        compiler_params=_cparams(("parallel",)),
        name=name,
    )(state, col(q), col(k), col(g), v.reshape(B, nh, 1, dv))
    return o.reshape(B, nh * dv), sn


PAGES_PER_STEP = 16


def _fox_decode_kernel(npg, n_pages, pt_ref, *refs):
    qrow_ref, krow_ref, qcol_ref, vcol_ref, lfnew_ref, u_ref, lf_ref = refs[:7]
    k_refs = refs[7:7 + npg]
    v_refs = refs[7 + npg:7 + 2 * npg]
    o_ref = refs[7 + 2 * npg]
    qb_scr, m_scr, l_scr, t_scr, acc_scr, s_scr, p_scr = refs[8 + 2 * npg:]
    j = pl.program_id(1)
    pt0 = pl.program_id(0) * n_pages + (pl.num_programs(1) - 1 - j) * npg
    nh, hd, page = qb_scr.shape

    @pl.when(j == 0)
    def _():
        qb_scr[...] = jnp.broadcast_to(qcol_ref[0], (nh, hd, page))
        m_scr[...] = jnp.sum(qrow_ref[0] * krow_ref[0], axis=1, keepdims=True)
        l_scr[...] = jnp.ones_like(l_scr)
        t_scr[...] = lfnew_ref[0]
        acc_scr[...] = jnp.where(_iota((nh, hd, page), 2) == 0,
                                 jnp.broadcast_to(vcol_ref[0], (nh, hd, page)), 0.0)

    u = u_ref[...]
    t_run = t_scr[...]
    for i in reversed(range(npg)):
        x = lf_ref[pt_ref[pt0 + i]]
        x1, x2, x3 = _split3(x)
        cat = jnp.concatenate([x1.astype(F32), x2.astype(F32), x3.astype(F32),
                               jnp.zeros_like(x)], axis=0).astype(BF16)
        y4 = _dot(cat, u)
        y = y4[0:nh] + y4[nh:2 * nh] + y4[2 * nh:3 * nh]
        for h in range(nh):
            s_scr[i, h:h + 1, :] = jnp.sum(k_refs[i][h] * qb_scr[h], axis=0, keepdims=True)
        s_scr[i] = s_scr[i] + (y + t_run)
        t_run = t_run + (y[:, 0:1] + x[:, 0:1])
    t_scr[...] = t_run
    s_all = s_scr[...]
    m_prev = m_scr[...]
    m_new = jnp.maximum(m_prev, jnp.max(jnp.max(s_all, axis=0), axis=1, keepdims=True))
    alpha = jnp.exp(m_prev - m_new)
    p_all = jnp.exp(s_all - m_new)
    p_scr[...] = p_all
    l_scr[...] = alpha * l_scr[...] + jnp.sum(jnp.sum(p_all, axis=0), axis=1, keepdims=True)
    m_scr[...] = m_new
    for h in range(nh):
        contrib = jnp.zeros((hd, page), F32)
        for i in range(npg):
            contrib = contrib + p_scr[i, h:h + 1, :] * v_refs[i][h]
        acc_scr[h] = acc_scr[h] * alpha[h:h + 1, :] + contrib

    @pl.when(j == pl.num_programs(1) - 1)
    def _():
        l = l_scr[...]
        for h in range(nh):
            o_ref[0, h] = jnp.sum(acc_scr[h], axis=1, keepdims=True) / l[h:h + 1, :]


def _fox_decode(q, knew, vnew, lfnew, cache_k, cache_v, cache_lf, page_table, layer):
    B, W = q.shape
    depth, n_pool, page, nh, hd = cache_k.shape
    n_pages = page_table.shape[1]
    npg = PAGES_PER_STEP
    assert n_pages % npg == 0
    ng = n_pages // npg
    ck = jnp.transpose(cache_k, (0, 1, 3, 4, 2))
    cv = jnp.transpose(cache_v, (0, 1, 3, 4, 2))
    clf = jnp.transpose(cache_lf, (0, 1, 3, 2))
    u = jnp.asarray(np.tril(np.ones((page, page), np.float32), -1), BF16)
    pt_flat = page_table.reshape(-1)

    def page_map(nd):
        def make(i):
            def index_map(b, j, pt):
                return (layer, pt[b * n_pages + (ng - 1 - j) * npg + i]) + (0,) * nd
            return index_map
        return make

    per_b = lambda shp: pl.BlockSpec((1,) + shp, lambda b, j, pt: (b,) + (0,) * len(shp))
    in_specs = [per_b((nh, hd)), per_b((nh, hd)), per_b((nh, hd, 1)), per_b((nh, hd, 1)),
                per_b((nh, 1)), pl.BlockSpec((page, page), lambda b, j, pt: (0, 0)),
                pl.BlockSpec((None, n_pool, nh, page), lambda b, j, pt: (layer, 0, 0, 0),
                             pipeline_mode=pl.Buffered(1))]
    in_specs += [pl.BlockSpec((None, None, nh, hd, page), page_map(3)(i)) for i in range(npg)]
    in_specs += [pl.BlockSpec((None, None, nh, hd, page), page_map(3)(i)) for i in range(npg)]
    out = pl.pallas_call(
        functools.partial(_fox_decode_kernel, npg, n_pages),
        grid_spec=pltpu.PrefetchScalarGridSpec(
            num_scalar_prefetch=1,
            grid=(B, ng),
            in_specs=in_specs,
            out_specs=pl.BlockSpec((1, nh, hd, 1), lambda b, j, pt: (b, 0, 0, 0)),
            scratch_shapes=[pltpu.VMEM((nh, hd, page), F32), pltpu.VMEM((nh, 1), F32),
                            pltpu.VMEM((nh, 1), F32), pltpu.VMEM((nh, 1), F32),
                            pltpu.VMEM((nh, hd, page), F32),
                            pltpu.VMEM((npg, nh, page), F32), pltpu.VMEM((npg, nh, page), F32)]),
        out_shape=jax.ShapeDtypeStruct((B, nh, hd, 1), F32),
        compiler_params=_cparams(("parallel", "arbitrary")),
        name="fox_decode",
    )(pt_flat, q.reshape(B, nh, hd), knew.reshape(B, nh, hd), q.reshape(B, nh, hd, 1),
      vnew.reshape(B, nh, hd, 1), lfnew.reshape(B, nh, 1), u,
      clf, *([ck] * npg), *([cv] * npg))
    return out.reshape(B, W)


def _prepare(attn_norm, w_in, fox_f_bias, gla_gate_w2, gla_gate_b, hgrn_lb_logits, hgrn_out_norm,
             gla_out_norm, w_out, ffn_norm, w_gate_up, w_down, final_norm):
    depth, D, in_width = w_in.shape
    n_fh = fox_f_bias.shape[1]
    main = 3328
    off_ff = 2560
    off_glr = in_width - GLA_GATE_RANK
    assert off_glr == off_ff + n_fh + (main - off_ff)
    w_main = jnp.concatenate([w_in[:, :, :off_ff], w_in[:, :, off_ff + n_fh:off_glr]], axis=2)
    misc = jnp.concatenate([w_in[:, :, off_ff:off_ff + n_fh], w_in[:, :, off_glr:],
                            jnp.zeros((depth, D, LANES - n_fh - GLA_GATE_RANK), w_in.dtype)], axis=2)
    w_p = jnp.concatenate([w_main, misc], axis=2).astype(BF16)
    assert w_p.shape[2] == W_PAD
    w_kvt = jnp.transpose(w_in[:, :, 1536:1536 + 2 * FOX_W], (0, 2, 1)).astype(BF16)
    fbias = jnp.zeros((depth, 1, LANES), F32).at[:, 0, :n_fh].set(fox_f_bias)
    w2pad = jnp.zeros((depth, LANES, LANES), F32).at[:, n_fh:n_fh + GLA_GATE_RANK, :].set(gla_gate_w2)
    ff = w_down.shape[1]
    tf = ff // 2 if (ff // 2) % LANES == 0 else ff
    n_hg_heads = 256 // HG_DV
    gsum = np.kron(np.eye(n_hg_heads, dtype=np.float32), np.full((HG_DV, HG_DV), 1.0 / HG_DV, np.float32))
    return dict(
        n_fox_heads=n_fh,
        attn_norm=attn_norm.reshape(depth, 1, D),
        w_in=w_p,
        w_kvt=w_kvt,
        fbias=fbias,
        lb_logits=hgrn_lb_logits,
        w2pad=w2pad.astype(BF16),
        gla_b=gla_gate_b.reshape(depth, 1, LANES),
        hgn=jnp.tile(hgrn_out_norm, (1, n_hg_heads)).reshape(depth, 1, 256),
        glan=jnp.tile(gla_out_norm, (1, 256 // GLA_DV)).reshape(depth, 1, 256),
        gsum=jnp.asarray(np.concatenate([gsum, gsum], axis=0), BF16),
        w_out=w_out.astype(BF16),
        ffn_norm=ffn_norm.reshape(depth, 1, D),
        w_gu=w_gate_up.astype(BF16),
        w_down=w_down.astype(BF16),
        ff_tile=tf,
        final_norm=final_norm.reshape(1, D),
    )


def kernel(x_prompt, x_sample, cache_fox_k, cache_fox_v, cache_fox_logf, state_hgrn, state_gla, page_table, attn_norm, w_in, fox_f_bias, gla_gate_w2, gla_gate_b, hgrn_lb_logits, hgrn_out_norm, gla_out_norm, w_out, ffn_norm, w_gate_up, w_down, final_norm):
    B, L, D = x_prompt.shape
    DB = x_sample.shape[0]
    depth = w_in.shape[0]
    n_fh = fox_f_bias.shape[1]
    hd = FOX_HEAD_DIM
    pw = _prepare(attn_norm, w_in, fox_f_bias, gla_gate_w2, gla_gate_b, hgrn_lb_logits,
                  hgrn_out_norm, gla_out_norm, w_out, ffn_norm, w_gate_up, w_down, final_norm)
    tm = 512
    M = B * L
    xp = x_prompt
    xs = x_sample.reshape(1, DB, D)
    kp, vp, lfp, hsp, gsp = [], [], [], [], []
    ksm, vsm, lfs, hss, gss = [], [], [], [], []
    r2 = lambda a: a.reshape(M, a.shape[-1])
    s2 = lambda a: a.reshape(DB, a.shape[-1])
    for l in range(depth):
        final = l == depth - 1
        a = _project(xp, l, True, tm, pw)
        o_hg, s_hg = _gla_prompt(a["hq"], a["hk"], a["hlf"], a["hv"], HG_DK, HG_DV, "hgrn_prompt")
        o_fox = _fox_prompt(a["fq"], a["fk_b"], a["fvt_b"], a["c"], tm)
        o_gla, s_gla = _gla_prompt(a["gq"], a["gk"], a["glg"], a["gv"], GLA_DK, GLA_DV, "gla_prompt")
        x1, h2 = _merge(r2(xp), r2(o_hg), r2(a["hgate"]), r2(o_fox), r2(o_gla), r2(a["ggate"]),
                        l, tm, pw)
        xp = _ffn(h2, x1, l, final, tm, pw).reshape(B, L, D)
        kp.append(jnp.transpose(a["fkt"].reshape(B, n_fh, hd, L), (0, 3, 1, 2)))
        vp.append(jnp.transpose(a["fvt"].reshape(B, n_fh, hd, L), (0, 3, 1, 2)))
        lfp.append(a["flf"])
        hsp.append(s_hg)
        gsp.append(s_gla)
        a = _project(xs, l, False, DB, pw)
        o_hg, s_hg = _lin_step(state_hgrn[l], s2(a["hq"]), s2(a["hk"]), s2(a["hlf"]), s2(a["hv"]),
                               "hgrn_step")
        o_fox = _fox_decode(s2(a["fq"]).astype(F32), s2(a["fk"]), s2(a["fv"]), s2(a["flf"]),
                            cache_fox_k, cache_fox_v, cache_fox_logf, page_table, l)
        o_gla, s_gla = _lin_step(state_gla[l], s2(a["gq"]), s2(a["gk"]), s2(a["glg"]), s2(a["gv"]),
                                 "gla_step")
        x1, h2 = _merge(s2(xs), o_hg, s2(a["hgate"]), o_fox, o_gla, s2(a["ggate"]), l, DB, pw)
        xs = _ffn(h2, x1, l, final, DB, pw).reshape(1, DB, D)
        ksm.append(a["fk"].reshape(DB, 1, n_fh, hd))
        vsm.append(a["fv"].reshape(DB, 1, n_fh, hd))
        lfs.append(a["flf"].reshape(DB, 1, n_fh))
        hss.append(s_hg)
        gss.append(s_gla)
    return (xp, xs.reshape(DB, 1, D),
            jnp.stack(kp), jnp.stack(vp), jnp.stack(lfp), jnp.stack(hsp), jnp.stack(gsp),
            jnp.stack(ksm), jnp.stack(vsm), jnp.stack(lfs), jnp.stack(hss), jnp.stack(gss))
```

```python
import functools

import numpy as np
import jax
import jax.numpy as jnp
from jax import lax
from jax.experimental import pallas as pl
from jax.experimental.pallas import tpu as pltpu

F32 = jnp.float32
BF16 = jnp.bfloat16

HG_DK = 64
HG_DV = 64
FOX_HEAD_DIM = 64
GLA_DK = 32
GLA_DV = 64
GLA_GATE_RANK = 16
GLA_GATE_NORMALIZER = 16.0
RMS_EPS = 1e-6
LOG2E = 1.4426950408889634

LANES = 128
CHUNK = 64
SUB = 16
NSUB = CHUNK // SUB
GLA_SEQS_PER_STEP = 8
NEG_BIG = -1e30
VMEM_LIMIT = 48 * 1024 * 1024


def _cparams(sem):
    return pltpu.CompilerParams(dimension_semantics=sem, vmem_limit_bytes=VMEM_LIMIT)


def _shift(n):
    s = int(np.log2(n))
    assert (1 << s) == n
    return s


def _iota(shape, dim):
    return lax.broadcasted_iota(jnp.int32, shape, dim)


def _split3(x):
    x1 = x.astype(BF16)
    r1 = x - x1.astype(F32)
    x2 = r1.astype(BF16)
    r2 = r1 - x2.astype(F32)
    return x1, x2, r2.astype(BF16)


def _log_sigmoid(x):
    return jnp.minimum(x, 0.0) - jnp.log1p(jnp.exp(-jnp.abs(x)))


def _sigmoid(x):
    return 1.0 / (1.0 + jnp.exp(-x))


def _dot(a, b):
    return jnp.dot(a, b, preferred_element_type=F32)


def _dot_nt(a, b):
    return lax.dot_general(a, b, (((1,), (1,)), ((), ())), preferred_element_type=F32)


def _dot_tn(a, b):
    return lax.dot_general(a, b, (((0,), (0,)), ((), ())), preferred_element_type=F32)


def _rms(x, w):
    ms = jnp.mean(x * x, axis=-1, keepdims=True)
    return x * lax.rsqrt(ms + RMS_EPS) * w


W_PAD = 3456
MISC = 3328
FOX_W = 512
FOX_VROWS = FOX_HEAD_DIM + 16
COMMON_OUT = ("hq", "hk", "hlf", "hv", "hgate", "fq", "flf", "gq", "gk", "gv", "glg", "ggate")
PROMPT_OUT = COMMON_OUT + ("fk_b", "fkt", "fvt", "fvt_b", "c")
SAMPLE_OUT = COMMON_OUT + ("fk", "fv")


def _proj_kernel(layer, prompt, *refs):
    (x_ref, norm_ref, w_ref, fbias_ref, lbl_ref, w2_ref, gb_ref) = refs[:7]
    k = 7
    if prompt:
        wkvt_ref, tri_ref = refs[7], refs[8]
        k = 9
        if layer > 0:
            prev_kt_ref, prev_vt_ref = refs[9], refs[10]
            k = 11
    names = PROMPT_OUT if prompt else SAMPLE_OUT
    o = dict(zip(names, refs[k:k + len(names)]))
    if prompt:
        carry_ref = refs[k + len(names)]

    hb = _rms(x_ref[0], norm_ref[...]).astype(BF16)

    def seg(a, b):
        return _dot(hb, w_ref[:, a:b])

    lbl = lbl_ref[...]
    e = jnp.exp(lbl - jnp.max(lbl, axis=0, keepdims=True))
    sm = e / jnp.sum(e, axis=0, keepdims=True)
    lb = jnp.zeros((1, lbl.shape[1]), F32)
    for i in range(1, layer + 1):
        lb = lb + sm[i:i + 1, :]
    o["hq"][0] = seg(0, 256)
    f = lb + (1.0 - lb) * _sigmoid(seg(256, 512))
    o["hlf"][0] = jnp.log(f)
    o["hk"][0] = 1.0 - f
    o["hv"][0] = seg(512, 768).astype(o["hv"].dtype)
    o["hgate"][0] = seg(768, 1024)
    q_scale = FOX_HEAD_DIM ** -0.5 * (LOG2E if prompt else 1.0)
    o["fq"][0] = (seg(1024, 1536) * q_scale).astype(BF16)
    if prompt:
        kvt = _dot_nt(wkvt_ref[...], hb)
        o["fk_b"][0] = kvt[0:FOX_W].T.astype(BF16)
        o["fkt"][layer, 0] = kvt[0:FOX_W]
        o["fvt"][layer, 0] = kvt[FOX_W:2 * FOX_W]
        for lp in range(layer):
            o["fkt"][lp, 0] = prev_kt_ref[lp, 0]
            o["fvt"][lp, 0] = prev_vt_ref[lp, 0]
        hd = FOX_HEAD_DIM
        ones = jnp.ones((FOX_VROWS - hd, kvt.shape[1]), BF16)
        for h in range(FOX_W // hd):
            r = h * FOX_VROWS
            o["fvt_b"][0, r:r + hd, :] = kvt[FOX_W + h * hd:FOX_W + (h + 1) * hd].astype(BF16)
            o["fvt_b"][0, r + hd:r + FOX_VROWS, :] = ones
    else:
        o["fk"][0] = seg(1536, 2048)
        o["fv"][0] = seg(2048, 2560)
    o["gq"][0] = seg(2560, 2688) * (GLA_DK ** -0.5)
    o["gk"][0] = seg(2688, 2816)
    o["gv"][0] = seg(2816, 3072).astype(o["gv"].dtype)
    o["ggate"][0] = seg(3072, 3328)
    misc = seg(MISC, W_PAD)
    lf_all = _log_sigmoid(misc + fbias_ref[...])
    n_fh = o["flf"].shape[-1]
    o["flf"][0] = lf_all[:, 0:n_fh]
    g_logit = _dot(misc.astype(BF16), w2_ref[...]) + gb_ref[...]
    o["glg"][0] = _log_sigmoid(g_logit) / GLA_GATE_NORMALIZER

    if prompt:
        @pl.when(pl.program_id(1) == 0)
        def _():
            carry_ref[...] = jnp.zeros_like(carry_ref)

        x1, x2, x3 = _split3(lf_all)
        tri = tri_ref[...]
        c_all = _dot(tri, x1) + _dot(tri, x2) + _dot(tri, x3) + carry_ref[...]
        o["c"][0] = c_all[:, 0:n_fh] * LOG2E
        tm = c_all.shape[0]
        carry_ref[...] = c_all[tm - 1:tm, :]


def _project(x3, layer, prompt, tm, pw, prev_kv=None):
    B, L, D = x3.shape
    n_fh = pw["n_fox_heads"]
    grid = (B, L // tm)
    row = lambda w: pl.BlockSpec((1, tm, w), lambda b, t: (b, t, 0))
    const = lambda shp: pl.BlockSpec(shp, lambda b, t: (0,) * len(shp))
    v_dtype = BF16 if prompt else F32
    rows = dict(hq=(256, F32), hk=(256, F32), hlf=(256, F32), hv=(256, v_dtype), hgate=(256, F32),
                fq=(FOX_W, BF16), flf=(n_fh, F32), gq=(128, F32), gk=(128, F32), gv=(256, v_dtype),
                glg=(128, F32), ggate=(256, F32), fk_b=(FOX_W, BF16), c=(n_fh, F32),
                fk=(FOX_W, F32), fv=(FOX_W, F32))
    cols = dict(fkt=F32, fvt=F32, fvt_b=BF16)
    in_arrays = [x3, pw["attn_norm"][layer], pw["w_in"][layer], pw["fbias"][layer],
                 pw["lb_logits"], pw["w2pad"][layer], pw["gla_b"][layer]]
    in_specs = [row(D), const((1, D)), const((D, W_PAD)), const((1, LANES)),
                const(pw["lb_logits"].shape), const((LANES, LANES)), const((1, LANES))]
    scratch = []
    if prompt:
        tri = jnp.asarray(np.tril(np.ones((tm, tm), np.float32)), BF16)
        in_arrays += [pw["w_kvt"][layer], tri]
        in_specs += [const((2 * FOX_W, D)), const((tm, tm))]
        scratch.append(pltpu.VMEM((1, LANES), F32))
        if layer > 0:
            in_arrays += list(prev_kv)
            in_specs += [pl.BlockSpec((layer, 1, FOX_W, tm), lambda b, t: (0, b, 0, t))] * 2
    names = PROMPT_OUT if prompt else SAMPLE_OUT
    out_shape, out_specs = [], []
    for n in names:
        if n == "fvt_b":
            nrow = FOX_W // FOX_HEAD_DIM * FOX_VROWS
            out_shape.append(jax.ShapeDtypeStruct((B, nrow, L), cols[n]))
            out_specs.append(pl.BlockSpec((1, nrow, tm), lambda b, t: (b, 0, t)))
        elif n in cols:
            out_shape.append(jax.ShapeDtypeStruct((layer + 1, B, FOX_W, L), cols[n]))
            out_specs.append(pl.BlockSpec((layer + 1, 1, FOX_W, tm), lambda b, t: (0, b, 0, t)))
        else:
            w, dt = rows[n]
            out_shape.append(jax.ShapeDtypeStruct((B, L, w), dt))
            out_specs.append(row(w))
    outs = pl.pallas_call(
        functools.partial(_proj_kernel, layer, prompt),
        grid=grid, in_specs=in_specs, out_specs=out_specs, out_shape=out_shape,
        scratch_shapes=scratch,
        compiler_params=_cparams(("parallel", "arbitrary")),
        name="proj_prompt" if prompt else "proj_sample",
    )(*in_arrays)
    return dict(zip(names, outs))


def _gla_chunk_kernel(dk, dv, n_chunks, q_ref, k_ref, g_ref, v_ref, tt_ref, o_ref, st_ref, s_scr):
    hdk = q_ref.shape[-1]
    hdv = v_ref.shape[-1]
    nh = hdk // dk
    C = CHUNK

    @pl.when(pl.program_id(1) == 0)
    def _():
        s_scr[...] = jnp.zeros_like(s_scr)

    row_k = _iota((C, hdk), 0)
    head_k = _iota((C, hdk), 1) >> _shift(dk)
    head_v = _iota((C, hdv), 1) >> _shift(dv)
    a_row = _iota((C, nh * C), 0)
    a_col = _iota((C, nh * C), 1) & (C - 1)
    causal = a_col <= a_row
    bd_mask = (_iota((hdv, hdk), 0) >> _shift(dv)) == (_iota((hdv, hdk), 1) >> _shift(dk))

    def stage_cumsum(bb, r0):
        g1, g2, g3 = _split3(g_ref[bb, pl.ds(r0, C), :])
        gcat = jnp.concatenate([g1, g2, g3], axis=0)
        return _dot(tt_ref[...], gcat)

    def stage_scores(bb, r0, bw):
        q = q_ref[bb, pl.ds(r0, C), :]
        k = k_ref[bb, pl.ds(r0, C), :]
        b = bw[0:C]
        w = bw[C:2 * C]
        qt = q * jnp.exp(w)
        qb = (q * jnp.exp(b)).astype(BF16)
        bend = b[C - 1:C, :]
        kdec = (k * jnp.exp(bend - b)).astype(BF16)
        lhs_parts = []
        kt_parts = []
        for i in range(NSUB):
            in_sub = (row_k >> _shift(SUB)) == i
            lhs_parts.append(jnp.where(in_sub, qt, 0.0).astype(BF16))
            if i == 0:
                arg = jnp.where(row_k < SUB, -b, NEG_BIG)
            else:
                arg = jnp.where(row_k < SUB * (i + 1), b[SUB * i - 1:SUB * i, :] - b, NEG_BIG)
            kt_parts.append(k * jnp.exp(arg))
        lhs = jnp.concatenate(lhs_parts, axis=1)
        rt_rows = []
        for h in range(nh):
            rt_rows.append(jnp.concatenate(
                [jnp.where(head_k == h, kt, 0.0).astype(BF16) for kt in kt_parts], axis=1))
        rt = jnp.concatenate(rt_rows, axis=0)
        a_all = _dot_nt(lhs, rt)
        return a_all, qb, kdec, bend

    def stage_output(bb, r0, a_all, qb, kdec, bend):
        v = v_ref[bb, pl.ds(r0, C), :].astype(BF16)
        a_m = jnp.where(causal, a_all, 0.0).astype(BF16)
        vbd = jnp.concatenate([jnp.where(head_v == h, v, 0).astype(BF16) for h in range(nh)], axis=0)
        s_t = s_scr[bb]
        o = _dot(a_m, vbd) + _dot_nt(qb, s_t.astype(BF16))
        upd = _dot_tn(v, kdec)
        return o, s_t * jnp.exp(bend) + jnp.where(bd_mask, upd, 0.0)

    def body(ci, carry):
        r0 = pl.multiple_of(ci * C, C)
        seqs = range(q_ref.shape[0])
        bws = [stage_cumsum(bb, r0) for bb in seqs]
        mids = [stage_scores(bb, r0, bws[bb]) for bb in seqs]
        res = [stage_output(bb, r0, *mids[bb]) for bb in seqs]
        o_ref[:, pl.ds(r0, C), :] = jnp.stack([r[0] for r in res], axis=0)
        s_scr[...] = jnp.stack([r[1] for r in res], axis=0)
        return carry

    lax.fori_loop(0, n_chunks, body, 0)

    @pl.when(pl.program_id(1) == pl.num_programs(1) - 1)
    def _():
        st_ref[...] = s_scr[...]


def _gla_prompt(q, k, g, v, dk, dv, name):
    B, L, hdk = q.shape
    hdv = v.shape[-1]
    nh = hdk // dk
    assert nh == hdv // dv and nh * CHUNK <= 256
    rows = 256 if L % 256 == 0 else L
    assert rows % CHUNK == 0 and L % rows == 0
    C = CHUNK
    t_full = np.tril(np.ones((C, C), np.float32))
    same_sub = (np.arange(C)[:, None] // SUB) == (np.arange(C)[None, :] // SUB)
    t_sub = t_full * same_sub
    tt = np.concatenate([np.tile(t_full, (1, 3)), np.tile(t_sub, (1, 3))], axis=0)
    tt = jnp.asarray(tt, BF16)
    nb = GLA_SEQS_PER_STEP if B % GLA_SEQS_PER_STEP == 0 else 1
    blk = lambda w: pl.BlockSpec((nb, rows, w), lambda b, t: (b, t, 0))
    o, st = pl.pallas_call(
        functools.partial(_gla_chunk_kernel, dk, dv, rows // C),
        grid=(B // nb, L // rows),
        in_specs=[blk(hdk), blk(hdk), blk(hdk), blk(hdv),
                  pl.BlockSpec((2 * C, 3 * C), lambda b, t: (0, 0))],
        out_specs=[blk(hdv), pl.BlockSpec((nb, hdv, hdk), lambda b, t: (b, 0, 0))],
        out_shape=[jax.ShapeDtypeStruct((B, L, hdv), F32),
                   jax.ShapeDtypeStruct((B, hdv, hdk), F32)],
        scratch_shapes=[pltpu.VMEM((nb, hdv, hdk), F32)],
        compiler_params=_cparams(("parallel", "arbitrary")),
        name=name,
    )(q, k, g, v, tt)
    st5 = st.reshape(B, nh, dv, nh, dk)
    idx = jnp.arange(nh)
    state = st5[:, idx, :, idx, :]
    return o, jnp.transpose(state, (1, 0, 3, 2))


FOX_HG = 4


def _fox_prompt_kernel(qi_ref, kj_ref, q_ref, k_ref, vt_ref, c_ref, o_ref, qm_scr, m_scr, acc_scr):
    i = qi_ref[pl.program_id(2)]
    j = kj_ref[pl.program_id(2)]
    tq = q_ref.shape[1]
    tk = k_ref.shape[1]
    w = q_ref.shape[2]
    hd = FOX_HEAD_DIM
    vr = FOX_VROWS

    @pl.when(j == 0)
    def _():
        q = q_ref[0]
        lane_head = _iota((tq, w), 1) >> _shift(hd)
        for h in range(FOX_HG):
            qm_scr[h] = jnp.where(lane_head == h, q, 0).astype(BF16)
        m_scr[...] = jnp.full_like(m_scr, -jnp.inf)
        acc_scr[...] = jnp.zeros_like(acc_scr)

    def compute(masked):
        k4 = k_ref[0]
        c = c_ref[0, 0]
        if masked:
            keep = _iota((tk, tq), 0) <= _iota((tk, tq), 1)
        def scores(h):
            s = _dot_nt(k4, qm_scr[h]) - c[:, h:h + 1]
            if masked:
                s = jnp.where(keep, s, -jnp.inf)
            return s, jnp.max(s, axis=0, keepdims=True)

        def softmax_pv(h, s, mx):
            m_prev = m_scr[h]
            m_new = jnp.maximum(m_prev, mx)
            alpha = jnp.exp2(m_prev - m_new)
            p = jnp.exp2(s - m_new)
            m_scr[h] = m_new
            pv = _dot(vt_ref[0, h * vr:(h + 1) * vr, :], p.astype(BF16))
            acc_scr[h] = acc_scr[h] * alpha + pv

        pending = [scores(h) for h in range(FOX_HG)]
        for h in range(FOX_HG):
            softmax_pv(h, *pending[h])

    @pl.when(j < i)
    def _():
        compute(False)

    @pl.when(j == i)
    def _():
        compute(True)
        ot = jnp.concatenate([acc_scr[h, 0:hd, :] / acc_scr[h, hd:hd + 1, :] for h in range(FOX_HG)],
                             axis=0)
        o_ref[0] = ot.T


def _fox_prompt(q, k, vt, c, tile):
    B, L, W = q.shape
    n_heads = c.shape[-1]
    hd = FOX_HEAD_DIM
    gw = FOX_HG * hd
    n_hg = W // gw
    nq = L // tile
    pairs = [(i, j) for i in range(nq) for j in range(i + 1)]
    qi = jnp.asarray([p[0] for p in pairs], jnp.int32)
    kj = jnp.asarray([p[1] for p in pairs], jnp.int32)
    qspec = pl.BlockSpec((1, tile, gw), lambda b, g, s, qi, kj: (b, qi[s], g))
    kspec = pl.BlockSpec((1, tile, gw), lambda b, g, s, qi, kj: (b, kj[s], g))
    vspec = pl.BlockSpec((1, FOX_HG * FOX_VROWS, tile), lambda b, g, s, qi, kj: (b, g, kj[s]))
    c4 = jnp.transpose(c.reshape(B, L, n_hg, FOX_HG), (0, 2, 1, 3))
    cspec = pl.BlockSpec((1, 1, tile, FOX_HG), lambda b, g, s, qi, kj: (b, g, kj[s], 0))
    return pl.pallas_call(
        _fox_prompt_kernel,
        grid_spec=pltpu.PrefetchScalarGridSpec(
            num_scalar_prefetch=2,
            grid=(B, n_hg, len(pairs)),
            in_specs=[qspec, kspec, vspec, cspec],
            out_specs=qspec,
            scratch_shapes=[pltpu.VMEM((FOX_HG, tile, gw), BF16),
                            pltpu.VMEM((FOX_HG, 1, tile), F32),
                            pltpu.VMEM((FOX_HG, FOX_VROWS, tile), F32)]),
        out_shape=jax.ShapeDtypeStruct((B, L, W), F32),
        compiler_params=_cparams(("parallel", "parallel", "arbitrary")),
        name="fox_prompt",
    )(qi, kj, q, k, vt, c4)


def _head_norm_gate(o, gate, w_row, gsum):
    y = o * o
    y1 = y.astype(BF16)
    y2 = (y - y1.astype(F32)).astype(BF16)
    ms = _dot(jnp.concatenate([y1, y2], axis=1), gsum)
    return o * lax.rsqrt(ms + RMS_EPS) * w_row * (gate * _sigmoid(gate))


def _merge_ffn_kernel(final, x_ref, ohg_ref, ghg_ref, ofox_ref, ogla_ref, ggla_ref, hgn_ref, glan_ref,
                      gsum_ref, wout_ref, ffn_norm_ref, wg_ref, wu_ref, wd_ref, fn_ref, o_ref,
                      acc_ref, h2_ref):
    j = pl.program_id(1)

    @pl.when(j == 0)
    def _():
        gsum = gsum_ref[...]
        hg = _head_norm_gate(ohg_ref[...], ghg_ref[...], hgn_ref[...], gsum)
        gla = _head_norm_gate(ogla_ref[...], ggla_ref[...], glan_ref[...], gsum)
        mix = jnp.concatenate([hg.astype(BF16), ofox_ref[...].astype(BF16), gla.astype(BF16)], axis=1)
        x1 = x_ref[...] + _dot(mix, wout_ref[...])
        acc_ref[...] = x1
        h2_ref[...] = _rms(x1, ffn_norm_ref[...]).astype(BF16)

    h = h2_ref[...]
    gate = _dot(h, wg_ref[...])
    up = _dot(h, wu_ref[...])
    act = (gate * _sigmoid(gate) * up).astype(BF16)
    acc_ref[...] += _dot(act, wd_ref[...])

    @pl.when(j == pl.num_programs(1) - 1)
    def _():
        x2 = acc_ref[...]
        o_ref[...] = _rms(x2, fn_ref[...]) if final else x2


def _merge_ffn(x, ohg, ghg, ofox, ogla, ggla, layer, final, tm, pw):
    M, D = x.shape
    ff = pw["w_down"][layer].shape[0]
    tf = pw["ff_tile"]
    nf = ff // tf
    row = lambda w: pl.BlockSpec((tm, w), lambda t, j: (t, 0))
    const = lambda shp: pl.BlockSpec(shp, lambda t, j: (0,) * len(shp))
    return pl.pallas_call(
        functools.partial(_merge_ffn_kernel, final),
        grid=(M // tm, nf),
        in_specs=[row(D), row(256), row(256), row(512), row(256), row(256),
                  const((1, 256)), const((1, 256)), const((512, 256)), const((D, D)), const((1, D)),
                  pl.BlockSpec((D, tf), lambda t, j: (0, j)),
                  pl.BlockSpec((D, tf), lambda t, j: (0, j + nf)),
                  pl.BlockSpec((tf, D), lambda t, j: (j, 0)),
                  const((1, D))],
        out_specs=row(D),
        out_shape=jax.ShapeDtypeStruct((M, D), F32),
        scratch_shapes=[pltpu.VMEM((tm, D), F32), pltpu.VMEM((tm, D), BF16)],
        compiler_params=_cparams(("parallel", "arbitrary")),
        name="merge_ffn",
    )(x, ohg, ghg, ofox, ogla, ggla, pw["hgn"][layer], pw["glan"][layer], pw["gsum"],
      pw["w_out"][layer], pw["ffn_norm"][layer], pw["w_gu"][layer], pw["w_gu"][layer],
      pw["w_down"][layer], pw["final_norm"])


LIN_STEP_SEQS = 8


def _lin_step_kernel(s_ref, q_ref, k_ref, g_ref, v_ref, o_ref, sn_ref):
    nb, nh = s_ref.shape[:2]
    for b in range(nb):
        for h in range(nh):
            s_new = jnp.exp(g_ref[b, h]) * s_ref[b, h] + k_ref[b, h] * v_ref[b, h]
            sn_ref[b, h] = s_new
            o_ref[b, h] = jnp.sum(q_ref[b, h] * s_new, axis=0, keepdims=True)


def _lin_step(state, q, k, g, v, name):
    B, nh, dk, dv = state.shape
    col = lambda a: a.reshape(B, nh, dk, 1)
    nb = LIN_STEP_SEQS if B % LIN_STEP_SEQS == 0 else 1
    cspec = pl.BlockSpec((nb, nh, dk, 1), lambda b: (b, 0, 0, 0))
    sspec = pl.BlockSpec((nb, nh, dk, dv), lambda b: (b, 0, 0, 0))
    vspec = pl.BlockSpec((nb, nh, 1, dv), lambda b: (b, 0, 0, 0))
    o, sn = pl.pallas_call(
        _lin_step_kernel,
        grid=(B // nb,),
        in_specs=[sspec, cspec, cspec, cspec, vspec],
        out_specs=[vspec, sspec],
        out_shape=[jax.ShapeDtypeStruct((B, nh, 1, dv), F32),
                   jax.ShapeDtypeStruct((B, nh, dk, dv), F32)],
        compiler_params=_cparams(("parallel",)),
        name=name,
    )(state, col(q), col(k), col(g), v.reshape(B, nh, 1, dv))
    return o.reshape(B, nh * dv), sn


PAGES_PER_STEP = 16


def _fox_decode_kernel(npg, n_pages, pt_ref, *refs):
    qrow_ref, krow_ref, qcol_ref, vcol_ref, lfnew_ref, u_ref, lf_ref = refs[:7]
    k_refs = refs[7:7 + npg]
    v_refs = refs[7 + npg:7 + 2 * npg]
    o_ref = refs[7 + 2 * npg]
    qb_scr, m_scr, l_scr, t_scr, acc_scr, s_scr, p_scr = refs[8 + 2 * npg:]
    j = pl.program_id(1)
    pt0 = pl.program_id(0) * n_pages + (pl.num_programs(1) - 1 - j) * npg
    nh, hd, page = qb_scr.shape

    @pl.when(j == 0)
    def _():
        qb_scr[...] = jnp.broadcast_to(qcol_ref[0], (nh, hd, page))
        m_scr[...] = jnp.sum(qrow_ref[0] * krow_ref[0], axis=1, keepdims=True)
        l_scr[...] = jnp.ones_like(l_scr)
        t_scr[...] = lfnew_ref[0]
        acc_scr[...] = jnp.where(_iota((nh, hd, page), 2) == 0,
                                 jnp.broadcast_to(vcol_ref[0], (nh, hd, page)), 0.0)

    u = u_ref[...]
    t_run = t_scr[...]
    for i in reversed(range(npg)):
        x = lf_ref[pt_ref[pt0 + i]]
        x1, x2, x3 = _split3(x)
        cat = jnp.concatenate([x1.astype(F32), x2.astype(F32), x3.astype(F32),
                               jnp.zeros_like(x)], axis=0).astype(BF16)
        y4 = _dot(cat, u)
        y = y4[0:nh] + y4[nh:2 * nh] + y4[2 * nh:3 * nh]
        for h in range(nh):
            s_scr[i, h:h + 1, :] = jnp.sum(k_refs[i][h] * qb_scr[h], axis=0, keepdims=True)
        s_scr[i] = s_scr[i] + (y + t_run)
        t_run = t_run + (y[:, 0:1] + x[:, 0:1])
    t_scr[...] = t_run
    s_all = s_scr[...]
    m_prev = m_scr[...]
    m_new = jnp.maximum(m_prev, jnp.max(jnp.max(s_all, axis=0), axis=1, keepdims=True))
    alpha = jnp.exp(m_prev - m_new)
    p_all = jnp.exp(s_all - m_new)
    p_scr[...] = p_all
    l_scr[...] = alpha * l_scr[...] + jnp.sum(jnp.sum(p_all, axis=0), axis=1, keepdims=True)
    m_scr[...] = m_new
    for h in range(nh):
        contrib = jnp.zeros((hd, page), F32)
        for i in range(npg):
            contrib = contrib + p_scr[i, h:h + 1, :] * v_refs[i][h]
        acc_scr[h] = acc_scr[h] * alpha[h:h + 1, :] + contrib

    @pl.when(j == pl.num_programs(1) - 1)
    def _():
        l = l_scr[...]
        for h in range(nh):
            o_ref[0, h] = jnp.sum(acc_scr[h], axis=1, keepdims=True) / l[h:h + 1, :]


def _fox_decode(q, knew, vnew, lfnew, cache_k, cache_v, cache_lf, page_table, layer):
    B, W = q.shape
    depth, n_pool, page, nh, hd = cache_k.shape
    n_pages = page_table.shape[1]
    npg = PAGES_PER_STEP
    assert n_pages % npg == 0
    ng = n_pages // npg
    ck = jnp.transpose(cache_k, (0, 1, 3, 4, 2))
    cv = jnp.transpose(cache_v, (0, 1, 3, 4, 2))
    clf = jnp.transpose(cache_lf, (0, 1, 3, 2))
    u = jnp.asarray(np.tril(np.ones((page, page), np.float32), -1), BF16)
    pt_flat = page_table.reshape(-1)

    def page_map(nd):
        def make(i):
            def index_map(b, j, pt):
                return (layer, pt[b * n_pages + (ng - 1 - j) * npg + i]) + (0,) * nd
            return index_map
        return make

    per_b = lambda shp: pl.BlockSpec((1,) + shp, lambda b, j, pt: (b,) + (0,) * len(shp))
    in_specs = [per_b((nh, hd)), per_b((nh, hd)), per_b((nh, hd, 1)), per_b((nh, hd, 1)),
                per_b((nh, 1)), pl.BlockSpec((page, page), lambda b, j, pt: (0, 0)),
                pl.BlockSpec((None, n_pool, nh, page), lambda b, j, pt: (layer, 0, 0, 0),
                             pipeline_mode=pl.Buffered(1))]
    in_specs += [pl.BlockSpec((None, None, nh, hd, page), page_map(3)(i)) for i in range(npg)]
    in_specs += [pl.BlockSpec((None, None, nh, hd, page), page_map(3)(i)) for i in range(npg)]
    out = pl.pallas_call(
        functools.partial(_fox_decode_kernel, npg, n_pages),
        grid_spec=pltpu.PrefetchScalarGridSpec(
            num_scalar_prefetch=1,
            grid=(B, ng),
            in_specs=in_specs,
            out_specs=pl.BlockSpec((1, nh, hd, 1), lambda b, j, pt: (b, 0, 0, 0)),
            scratch_shapes=[pltpu.VMEM((nh, hd, page), F32), pltpu.VMEM((nh, 1), F32),
                            pltpu.VMEM((nh, 1), F32), pltpu.VMEM((nh, 1), F32),
                            pltpu.VMEM((nh, hd, page), F32),
                            pltpu.VMEM((npg, nh, page), F32), pltpu.VMEM((npg, nh, page), F32)]),
        out_shape=jax.ShapeDtypeStruct((B, nh, hd, 1), F32),
        compiler_params=_cparams(("parallel", "arbitrary")),
        name="fox_decode",
    )(pt_flat, q.reshape(B, nh, hd), knew.reshape(B, nh, hd), q.reshape(B, nh, hd, 1),
      vnew.reshape(B, nh, hd, 1), lfnew.reshape(B, nh, 1), u,
      clf, *([ck] * npg), *([cv] * npg))
    return out.reshape(B, W)


def _prepare(attn_norm, w_in, fox_f_bias, gla_gate_w2, gla_gate_b, hgrn_lb_logits, hgrn_out_norm,
             gla_out_norm, w_out, ffn_norm, w_gate_up, w_down, final_norm):
    depth, D, in_width = w_in.shape
    n_fh = fox_f_bias.shape[1]
    main = 3328
    off_ff = 2560
    off_glr = in_width - GLA_GATE_RANK
    assert off_glr == off_ff + n_fh + (main - off_ff)
    w_main = jnp.concatenate([w_in[:, :, :off_ff], w_in[:, :, off_ff + n_fh:off_glr]], axis=2)
    misc = jnp.concatenate([w_in[:, :, off_ff:off_ff + n_fh], w_in[:, :, off_glr:],
                            jnp.zeros((depth, D, LANES - n_fh - GLA_GATE_RANK), w_in.dtype)], axis=2)
    w_p = jnp.concatenate([w_main, misc], axis=2).astype(BF16)
    assert w_p.shape[2] == W_PAD
    w_kvt = jnp.transpose(w_in[:, :, 1536:1536 + 2 * FOX_W], (0, 2, 1)).astype(BF16)
    fbias = jnp.zeros((depth, 1, LANES), F32).at[:, 0, :n_fh].set(fox_f_bias)
    w2pad = jnp.zeros((depth, LANES, LANES), F32).at[:, n_fh:n_fh + GLA_GATE_RANK, :].set(gla_gate_w2)
    ff = w_down.shape[1]
    tf = ff // 2 if (ff // 2) % LANES == 0 else ff
    n_hg_heads = 256 // HG_DV
    gsum = np.kron(np.eye(n_hg_heads, dtype=np.float32), np.full((HG_DV, HG_DV), 1.0 / HG_DV, np.float32))
    return dict(
        n_fox_heads=n_fh,
        attn_norm=attn_norm.reshape(depth, 1, D),
        w_in=w_p,
        w_kvt=w_kvt,
        fbias=fbias,
        lb_logits=hgrn_lb_logits,
        w2pad=w2pad.astype(BF16),
        gla_b=gla_gate_b.reshape(depth, 1, LANES),
        hgn=jnp.tile(hgrn_out_norm, (1, n_hg_heads)).reshape(depth, 1, 256),
        glan=jnp.tile(gla_out_norm, (1, 256 // GLA_DV)).reshape(depth, 1, 256),
        gsum=jnp.asarray(np.concatenate([gsum, gsum], axis=0), BF16),
        w_out=w_out.astype(BF16),
        ffn_norm=ffn_norm.reshape(depth, 1, D),
        w_gu=w_gate_up.astype(BF16),
        w_down=w_down.astype(BF16),
        ff_tile=tf,
        final_norm=final_norm.reshape(1, D),
    )


def kernel(x_prompt, x_sample, cache_fox_k, cache_fox_v, cache_fox_logf, state_hgrn, state_gla, page_table, attn_norm, w_in, fox_f_bias, gla_gate_w2, gla_gate_b, hgrn_lb_logits, hgrn_out_norm, gla_out_norm, w_out, ffn_norm, w_gate_up, w_down, final_norm):
    B, L, D = x_prompt.shape
    DB = x_sample.shape[0]
    depth = w_in.shape[0]
    n_fh = fox_f_bias.shape[1]
    hd = FOX_HEAD_DIM
    pw = _prepare(attn_norm, w_in, fox_f_bias, gla_gate_w2, gla_gate_b, hgrn_lb_logits,
                  hgrn_out_norm, gla_out_norm, w_out, ffn_norm, w_gate_up, w_down, final_norm)
    tm = 512
    M = B * L
    xp = x_prompt
    xs = x_sample.reshape(1, DB, D)
    prev_kv = None
    lfp, hsp, gsp = [], [], []
    ksm, vsm, lfs, hss, gss = [], [], [], [], []
    r2 = lambda a: a.reshape(M, a.shape[-1])
    s2 = lambda a: a.reshape(DB, a.shape[-1])
    for l in range(depth):
        final = l == depth - 1
        a = _project(xp, l, True, tm, pw, prev_kv)
        prev_kv = (a["fkt"], a["fvt"])
        o_hg, s_hg = _gla_prompt(a["hq"], a["hk"], a["hlf"], a["hv"], HG_DK, HG_DV, "hgrn_prompt")
        o_fox = _fox_prompt(a["fq"], a["fk_b"], a["fvt_b"], a["c"], tm)
        o_gla, s_gla = _gla_prompt(a["gq"], a["gk"], a["glg"], a["gv"], GLA_DK, GLA_DV, "gla_prompt")
        xp = _merge_ffn(r2(xp), r2(o_hg), r2(a["hgate"]), r2(o_fox), r2(o_gla), r2(a["ggate"]),
                        l, final, tm, pw).reshape(B, L, D)
        lfp.append(a["flf"])
        hsp.append(s_hg)
        gsp.append(s_gla)
        a = _project(xs, l, False, DB, pw)
        o_hg, s_hg = _lin_step(state_hgrn[l], s2(a["hq"]), s2(a["hk"]), s2(a["hlf"]), s2(a["hv"]),
                               "hgrn_step")
        o_fox = _fox_decode(s2(a["fq"]).astype(F32), s2(a["fk"]), s2(a["fv"]), s2(a["flf"]),
                            cache_fox_k, cache_fox_v, cache_fox_logf, page_table, l)
        o_gla, s_gla = _lin_step(state_gla[l], s2(a["gq"]), s2(a["gk"]), s2(a["glg"]), s2(a["gv"]),
                                 "gla_step")
        xs = _merge_ffn(s2(xs), o_hg, s2(a["hgate"]), o_fox, o_gla, s2(a["ggate"]),
                        l, final, DB, pw).reshape(1, DB, D)
        ksm.append(a["fk"].reshape(DB, 1, n_fh, hd))
        vsm.append(a["fv"].reshape(DB, 1, n_fh, hd))
        lfs.append(a["flf"].reshape(DB, 1, n_fh))
        hss.append(s_hg)
        gss.append(s_gla)
    kp_all, vp_all = (jnp.transpose(t.reshape(depth, B, n_fh, hd, L), (0, 1, 4, 2, 3)) for t in prev_kv)
    return (xp, xs.reshape(DB, 1, D),
            kp_all, vp_all, jnp.stack(lfp), jnp.stack(hsp), jnp.stack(gsp),
            jnp.stack(ksm), jnp.stack(vsm), jnp.stack(lfs), jnp.stack(hss), jnp.stack(gss))
```

```python
import functools

import numpy as np
import jax
import jax.numpy as jnp
from jax import lax
from jax.experimental import pallas as pl
from jax.experimental.pallas import tpu as pltpu

F32 = jnp.float32
BF16 = jnp.bfloat16

HG_DK = 64
HG_DV = 64
FOX_HEAD_DIM = 64
GLA_DK = 32
GLA_DV = 64
GLA_GATE_RANK = 16
GLA_GATE_NORMALIZER = 16.0
RMS_EPS = 1e-6
LOG2E = 1.4426950408889634

LANES = 128
CHUNK = 64
SUB = 32
NSUB = CHUNK // SUB
GLA_SEQS_PER_STEP = 8
NEG_BIG = -1e30
VMEM_LIMIT = 48 * 1024 * 1024


def _cparams(sem):
    return pltpu.CompilerParams(dimension_semantics=sem, vmem_limit_bytes=VMEM_LIMIT)


def _shift(n):
    s = int(np.log2(n))
    assert (1 << s) == n
    return s


def _iota(shape, dim):
    return lax.broadcasted_iota(jnp.int32, shape, dim)


def _split3(x):
    x1 = x.astype(BF16)
    r1 = x - x1.astype(F32)
    x2 = r1.astype(BF16)
    r2 = r1 - x2.astype(F32)
    return x1, x2, r2.astype(BF16)


def _log_sigmoid(x):
    return jnp.minimum(x, 0.0) - jnp.log1p(jnp.exp(-jnp.abs(x)))


def _sigmoid(x):
    return 1.0 / (1.0 + jnp.exp(-x))


def _dot(a, b):
    return jnp.dot(a, b, preferred_element_type=F32)


def _dot_nt(a, b):
    return lax.dot_general(a, b, (((1,), (1,)), ((), ())), preferred_element_type=F32)


def _dot_tn(a, b):
    return lax.dot_general(a, b, (((0,), (0,)), ((), ())), preferred_element_type=F32)


def _rms(x, w):
    ms = jnp.mean(x * x, axis=-1, keepdims=True)
    return x * lax.rsqrt(ms + RMS_EPS) * w


W_PAD = 3456
MISC = 3328
FOX_W = 512
FOX_VROWS = FOX_HEAD_DIM + 16
COMMON_OUT = ("hq", "hk", "hlf", "hv", "hgate", "fq", "flf", "gq", "gk", "gv", "glg", "ggate")
PROMPT_OUT = COMMON_OUT + ("fk_b", "fkt", "fvt", "fvt_b", "c")
SAMPLE_OUT = COMMON_OUT + ("fk", "fv")


def _proj_kernel(layer, prompt, *refs):
    (x_ref, norm_ref, w_ref, fbias_ref, lbl_ref, w2_ref, gb_ref) = refs[:7]
    k = 7
    if prompt:
        wkvt_ref, tri_ref = refs[7], refs[8]
        k = 9
        if layer > 0:
            prev_kt_ref, prev_vt_ref = refs[9], refs[10]
            k = 11
    names = PROMPT_OUT if prompt else SAMPLE_OUT
    o = dict(zip(names, refs[k:k + len(names)]))
    if prompt:
        carry_ref = refs[k + len(names)]

    hb = _rms(x_ref[0], norm_ref[...]).astype(BF16)

    def seg(a, b):
        return _dot(hb, w_ref[:, a:b])

    lbl = lbl_ref[...]
    e = jnp.exp(lbl - jnp.max(lbl, axis=0, keepdims=True))
    sm = e / jnp.sum(e, axis=0, keepdims=True)
    lb = jnp.zeros((1, lbl.shape[1]), F32)
    for i in range(1, layer + 1):
        lb = lb + sm[i:i + 1, :]
    o["hq"][0] = seg(0, 256)
    f = lb + (1.0 - lb) * _sigmoid(seg(256, 512))
    o["hlf"][0] = jnp.log(f)
    o["hk"][0] = 1.0 - f
    o["hv"][0] = seg(512, 768).astype(o["hv"].dtype)
    o["hgate"][0] = seg(768, 1024)
    q_scale = FOX_HEAD_DIM ** -0.5 * (LOG2E if prompt else 1.0)
    o["fq"][0] = (seg(1024, 1536) * q_scale).astype(BF16)
    if prompt:
        kvt = _dot_nt(wkvt_ref[...], hb)
        o["fk_b"][0] = kvt[0:FOX_W].T.astype(BF16)
        o["fkt"][layer, 0] = kvt[0:FOX_W]
        o["fvt"][layer, 0] = kvt[FOX_W:2 * FOX_W]
        for lp in range(layer):
            o["fkt"][lp, 0] = prev_kt_ref[lp, 0]
            o["fvt"][lp, 0] = prev_vt_ref[lp, 0]
        hd = FOX_HEAD_DIM
        ones = jnp.ones((FOX_VROWS - hd, kvt.shape[1]), BF16)
        for h in range(FOX_W // hd):
            r = h * FOX_VROWS
            o["fvt_b"][0, r:r + hd, :] = kvt[FOX_W + h * hd:FOX_W + (h + 1) * hd].astype(BF16)
            o["fvt_b"][0, r + hd:r + FOX_VROWS, :] = ones
    else:
        o["fk"][0] = seg(1536, 2048)
        o["fv"][0] = seg(2048, 2560)
    o["gq"][0] = seg(2560, 2688) * (GLA_DK ** -0.5)
    o["gk"][0] = seg(2688, 2816)
    o["gv"][0] = seg(2816, 3072).astype(o["gv"].dtype)
    o["ggate"][0] = seg(3072, 3328)
    misc = seg(MISC, W_PAD)
    lf_all = _log_sigmoid(misc + fbias_ref[...])
    n_fh = o["flf"].shape[-1]
    o["flf"][0] = lf_all[:, 0:n_fh]
    g_logit = _dot(misc.astype(BF16), w2_ref[...]) + gb_ref[...]
    o["glg"][0] = _log_sigmoid(g_logit) / GLA_GATE_NORMALIZER

    if prompt:
        @pl.when(pl.program_id(1) == 0)
        def _():
            carry_ref[...] = jnp.zeros_like(carry_ref)

        x1, x2, x3 = _split3(lf_all)
        tri = tri_ref[...]
        c_all = _dot(tri, x1) + _dot(tri, x2) + _dot(tri, x3) + carry_ref[...]
        for g in range(n_fh // FOX_HG):
            o["c"][0, g] = c_all[:, g * FOX_HG:(g + 1) * FOX_HG] * LOG2E
        tm = c_all.shape[0]
        carry_ref[...] = c_all[tm - 1:tm, :]


def _project(x3, layer, prompt, tm, pw, prev_kv=None):
    B, L, D = x3.shape
    n_fh = pw["n_fox_heads"]
    grid = (B, L // tm)
    row = lambda w: pl.BlockSpec((1, tm, w), lambda b, t: (b, t, 0))
    const = lambda shp: pl.BlockSpec(shp, lambda b, t: (0,) * len(shp))
    v_dtype = BF16 if prompt else F32
    rows = dict(hq=(256, F32), hk=(256, F32), hlf=(256, F32), hv=(256, v_dtype), hgate=(256, F32),
                fq=(FOX_W, BF16), flf=(n_fh, F32), gq=(128, F32), gk=(128, F32), gv=(256, v_dtype),
                glg=(128, F32), ggate=(256, F32), fk_b=(FOX_W, BF16), c=(n_fh, F32),
                fk=(FOX_W, F32), fv=(FOX_W, F32))
    cols = dict(fkt=F32, fvt=F32, fvt_b=BF16)
    in_arrays = [x3, pw["attn_norm"][layer], pw["w_in"][layer], pw["fbias"][layer],
                 pw["lb_logits"], pw["w2pad"][layer], pw["gla_b"][layer]]
    in_specs = [row(D), const((1, D)), const((D, W_PAD)), const((1, LANES)),
                const(pw["lb_logits"].shape), const((LANES, LANES)), const((1, LANES))]
    scratch = []
    if prompt:
        tri = jnp.asarray(np.tril(np.ones((tm, tm), np.float32)), BF16)
        in_arrays += [pw["w_kvt"][layer], tri]
        in_specs += [const((2 * FOX_W, D)), const((tm, tm))]
        scratch.append(pltpu.VMEM((1, LANES), F32))
        if layer > 0:
            in_arrays += list(prev_kv)
            in_specs += [pl.BlockSpec((layer, 1, FOX_W, tm), lambda b, t: (0, b, 0, t))] * 2
    names = PROMPT_OUT if prompt else SAMPLE_OUT
    out_shape, out_specs = [], []
    for n in names:
        if n == "fvt_b":
            nrow = FOX_W // FOX_HEAD_DIM * FOX_VROWS
            out_shape.append(jax.ShapeDtypeStruct((B, nrow, L), cols[n]))
            out_specs.append(pl.BlockSpec((1, nrow, tm), lambda b, t: (b, 0, t)))
        elif n in cols:
            out_shape.append(jax.ShapeDtypeStruct((layer + 1, B, FOX_W, L), cols[n]))
            out_specs.append(pl.BlockSpec((layer + 1, 1, FOX_W, tm), lambda b, t: (0, b, 0, t)))
        elif n == "c":
            n_hg = n_fh // FOX_HG
            out_shape.append(jax.ShapeDtypeStruct((B, n_hg, L, FOX_HG), F32))
            out_specs.append(pl.BlockSpec((1, n_hg, tm, FOX_HG), lambda b, t: (b, 0, t, 0)))
        else:
            w, dt = rows[n]
            out_shape.append(jax.ShapeDtypeStruct((B, L, w), dt))
            out_specs.append(row(w))
    outs = pl.pallas_call(
        functools.partial(_proj_kernel, layer, prompt),
        grid=grid, in_specs=in_specs, out_specs=out_specs, out_shape=out_shape,
        scratch_shapes=scratch,
        compiler_params=_cparams(("parallel", "arbitrary")),
        name="proj_prompt" if prompt else "proj_sample",
    )(*in_arrays)
    return dict(zip(names, outs))


def _gla_chunk_kernel(dk, dv, n_chunks, q_ref, k_ref, g_ref, v_ref, tt_ref, o_ref, st_ref, s_scr):
    hdk = q_ref.shape[-1]
    hdv = v_ref.shape[-1]
    nh = hdk // dk
    C = CHUNK

    @pl.when(pl.program_id(1) == 0)
    def _():
        s_scr[...] = jnp.zeros_like(s_scr)

    row_k = _iota((C, hdk), 0)
    head_k = _iota((C, hdk), 1) >> _shift(dk)
    head_v = _iota((C, hdv), 1) >> _shift(dv)
    a_row = _iota((C, nh * C), 0)
    a_col = _iota((C, nh * C), 1) & (C - 1)
    causal = a_col <= a_row
    bd_mask = (_iota((hdv, hdk), 0) >> _shift(dv)) == (_iota((hdv, hdk), 1) >> _shift(dk))

    def stage_cumsum(bb, r0):
        g1, g2, g3 = _split3(g_ref[bb, pl.ds(r0, C), :])
        gcat = jnp.concatenate([g1, g2, g3], axis=0)
        return _dot(tt_ref[...], gcat)

    def stage_scores(bb, r0, bw):
        q = q_ref[bb, pl.ds(r0, C), :]
        k = k_ref[bb, pl.ds(r0, C), :]
        b = bw[0:C]
        w = bw[C:2 * C]
        qt = q * jnp.exp(w)
        qb = (q * jnp.exp(b)).astype(BF16)
        bend = b[C - 1:C, :]
        kdec = (k * jnp.exp(bend - b)).astype(BF16)
        lhs_parts = []
        kt_parts = []
        for i in range(NSUB):
            in_sub = (row_k >> _shift(SUB)) == i
            lhs_parts.append(jnp.where(in_sub, qt, 0.0).astype(BF16))
            if i == 0:
                arg = jnp.where(row_k < SUB, -b, NEG_BIG)
            else:
                arg = jnp.where(row_k < SUB * (i + 1), b[SUB * i - 1:SUB * i, :] - b, NEG_BIG)
            kt_parts.append(k * jnp.exp(arg))
        lhs = jnp.concatenate(lhs_parts, axis=1)
        rt_rows = []
        for h in range(nh):
            rt_rows.append(jnp.concatenate(
                [jnp.where(head_k == h, kt, 0.0).astype(BF16) for kt in kt_parts], axis=1))
        rt = jnp.concatenate(rt_rows, axis=0)
        a_all = _dot_nt(lhs, rt)
        return a_all, qb, kdec, bend

    def stage_output(bb, r0, a_all, qb, kdec, bend):
        v = v_ref[bb, pl.ds(r0, C), :].astype(BF16)
        a_m = jnp.where(causal, a_all, 0.0).astype(BF16)
        vbd = jnp.concatenate([jnp.where(head_v == h, v, 0).astype(BF16) for h in range(nh)], axis=0)
        s_t = s_scr[bb]
        o = _dot(a_m, vbd) + _dot_nt(qb, s_t.astype(BF16))
        upd = _dot_tn(v, kdec)
        return o, s_t * jnp.exp(bend) + jnp.where(bd_mask, upd, 0.0)

    def body(ci, carry):
        r0 = pl.multiple_of(ci * C, C)
        seqs = range(q_ref.shape[0])
        bws = [stage_cumsum(bb, r0) for bb in seqs]
        mids = [stage_scores(bb, r0, bws[bb]) for bb in seqs]
        res = [stage_output(bb, r0, *mids[bb]) for bb in seqs]
        o_ref[:, pl.ds(r0, C), :] = jnp.stack([r[0] for r in res], axis=0)
        s_scr[...] = jnp.stack([r[1] for r in res], axis=0)
        return carry

    lax.fori_loop(0, n_chunks, body, 0)

    @pl.when(pl.program_id(1) == pl.num_programs(1) - 1)
    def _():
        st_ref[...] = s_scr[...]


def _gla_prompt(q, k, g, v, dk, dv, name):
    B, L, hdk = q.shape
    hdv = v.shape[-1]
    nh = hdk // dk
    assert nh == hdv // dv and nh * CHUNK <= 256
    rows = 256 if L % 256 == 0 else L
    assert rows % CHUNK == 0 and L % rows == 0
    C = CHUNK
    t_full = np.tril(np.ones((C, C), np.float32))
    same_sub = (np.arange(C)[:, None] // SUB) == (np.arange(C)[None, :] // SUB)
    t_sub = t_full * same_sub
    tt = np.concatenate([np.tile(t_full, (1, 3)), np.tile(t_sub, (1, 3))], axis=0)
    tt = jnp.asarray(tt, BF16)
    nb = GLA_SEQS_PER_STEP if B % GLA_SEQS_PER_STEP == 0 else 1
    blk = lambda w: pl.BlockSpec((nb, rows, w), lambda b, t: (b, t, 0))
    o, st = pl.pallas_call(
        functools.partial(_gla_chunk_kernel, dk, dv, rows // C),
        grid=(B // nb, L // rows),
        in_specs=[blk(hdk), blk(hdk), blk(hdk), blk(hdv),
                  pl.BlockSpec((2 * C, 3 * C), lambda b, t: (0, 0))],
        out_specs=[blk(hdv), pl.BlockSpec((nb, hdv, hdk), lambda b, t: (b, 0, 0))],
        out_shape=[jax.ShapeDtypeStruct((B, L, hdv), F32),
                   jax.ShapeDtypeStruct((B, hdv, hdk), F32)],
        scratch_shapes=[pltpu.VMEM((nb, hdv, hdk), F32)],
        compiler_params=_cparams(("parallel", "arbitrary")),
        name=name,
    )(q, k, g, v, tt)
    st5 = st.reshape(B, nh, dv, nh, dk)
    idx = jnp.arange(nh)
    state = st5[:, idx, :, idx, :]
    return o, jnp.transpose(state, (1, 0, 3, 2))


FOX_HG = 4


def _fox_prompt_kernel(qi_ref, kj_ref, q_ref, k_ref, vt_ref, c_ref, o_ref, qm_scr, m_scr, acc_scr):
    i = qi_ref[pl.program_id(2)]
    j = kj_ref[pl.program_id(2)]
    tq = q_ref.shape[1]
    tk = k_ref.shape[1]
    w = q_ref.shape[2]
    hd = FOX_HEAD_DIM
    vr = FOX_VROWS

    @pl.when(j == 0)
    def _():
        q = q_ref[0]
        lane_head = _iota((tq, w), 1) >> _shift(hd)
        for h in range(FOX_HG):
            qm_scr[h] = jnp.where(lane_head == h, q, 0).astype(BF16)
        m_scr[...] = jnp.full_like(m_scr, -jnp.inf)
        acc_scr[...] = jnp.zeros_like(acc_scr)

    def compute(masked):
        k4 = k_ref[0]
        c = c_ref[0, 0]
        if masked:
            keep = _iota((tk, tq), 0) <= _iota((tk, tq), 1)
        def scores(h):
            s = _dot_nt(k4, qm_scr[h]) - c[:, h:h + 1]
            if masked:
                s = jnp.where(keep, s, -jnp.inf)
            return s, jnp.max(s, axis=0, keepdims=True)

        def softmax_pv(h, s, mx):
            m_prev = m_scr[h]
            m_new = jnp.maximum(m_prev, mx)
            alpha = jnp.exp2(m_prev - m_new)
            p = jnp.exp2(s - m_new)
            m_scr[h] = m_new
            pv = _dot(vt_ref[0, h * vr:(h + 1) * vr, :], p.astype(BF16))
            acc_scr[h] = acc_scr[h] * alpha + pv

        pending = [scores(h) for h in range(FOX_HG)]
        for h in range(FOX_HG):
            softmax_pv(h, *pending[h])

    @pl.when(j < i)
    def _():
        compute(False)

    @pl.when(j == i)
    def _():
        compute(True)
        ot = jnp.concatenate([acc_scr[h, 0:hd, :] / acc_scr[h, hd:hd + 1, :] for h in range(FOX_HG)],
                             axis=0)
        o_ref[0] = ot.T


def _fox_prompt(q, k, vt, c, tile):
    B, L, W = q.shape
    hd = FOX_HEAD_DIM
    gw = FOX_HG * hd
    n_hg = W // gw
    nq = L // tile
    pairs = [(i, j) for i in range(nq) for j in range(i + 1)]
    qi = jnp.asarray([p[0] for p in pairs], jnp.int32)
    kj = jnp.asarray([p[1] for p in pairs], jnp.int32)
    qspec = pl.BlockSpec((1, tile, gw), lambda b, g, s, qi, kj: (b, qi[s], g))
    kspec = pl.BlockSpec((1, tile, gw), lambda b, g, s, qi, kj: (b, kj[s], g))
    vspec = pl.BlockSpec((1, FOX_HG * FOX_VROWS, tile), lambda b, g, s, qi, kj: (b, g, kj[s]))
    c4 = c
    cspec = pl.BlockSpec((1, 1, tile, FOX_HG), lambda b, g, s, qi, kj: (b, g, kj[s], 0))
    return pl.pallas_call(
        _fox_prompt_kernel,
        grid_spec=pltpu.PrefetchScalarGridSpec(
            num_scalar_prefetch=2,
            grid=(B, n_hg, len(pairs)),
            in_specs=[qspec, kspec, vspec, cspec],
            out_specs=qspec,
            scratch_shapes=[pltpu.VMEM((FOX_HG, tile, gw), BF16),
                            pltpu.VMEM((FOX_HG, 1, tile), F32),
                            pltpu.VMEM((FOX_HG, FOX_VROWS, tile), F32)]),
        out_shape=jax.ShapeDtypeStruct((B, L, W), F32),
        compiler_params=_cparams(("parallel", "parallel", "arbitrary")),
        name="fox_prompt",
    )(qi, kj, q, k, vt, c4)


def _head_norm_gate(o, gate, w_row, gsum):
    y = o * o
    y1 = y.astype(BF16)
    y2 = (y - y1.astype(F32)).astype(BF16)
    ms = _dot(jnp.concatenate([y1, y2], axis=1), gsum)
    return o * lax.rsqrt(ms + RMS_EPS) * w_row * (gate * _sigmoid(gate))


def _merge_ffn_kernel(final, x_ref, ohg_ref, ghg_ref, ofox_ref, ogla_ref, ggla_ref, hgn_ref, glan_ref,
                      gsum_ref, wout_ref, ffn_norm_ref, wg_ref, wu_ref, wd_ref, fn_ref, o_ref,
                      acc_ref, h2_ref):
    j = pl.program_id(1)

    @pl.when(j == 0)
    def _():
        gsum = gsum_ref[...]
        hg = _head_norm_gate(ohg_ref[...], ghg_ref[...], hgn_ref[...], gsum)
        gla = _head_norm_gate(ogla_ref[...], ggla_ref[...], glan_ref[...], gsum)
        mix = jnp.concatenate([hg.astype(BF16), ofox_ref[...].astype(BF16), gla.astype(BF16)], axis=1)
        x1 = x_ref[...] + _dot(mix, wout_ref[...])
        acc_ref[...] = x1
        h2_ref[...] = _rms(x1, ffn_norm_ref[...]).astype(BF16)

    h = h2_ref[...]
    gate = _dot(h, wg_ref[...])
    up = _dot(h, wu_ref[...])
    act = (gate * _sigmoid(gate) * up).astype(BF16)
    acc_ref[...] += _dot(act, wd_ref[...])

    @pl.when(j == pl.num_programs(1) - 1)
    def _():
        x2 = acc_ref[...]
        o_ref[...] = _rms(x2, fn_ref[...]) if final else x2


def _merge_ffn(x, ohg, ghg, ofox, ogla, ggla, layer, final, tm, pw):
    M, D = x.shape
    ff = pw["w_down"][layer].shape[0]
    tf = pw["ff_tile"]
    nf = ff // tf
    row = lambda w: pl.BlockSpec((tm, w), lambda t, j: (t, 0))
    const = lambda shp: pl.BlockSpec(shp, lambda t, j: (0,) * len(shp))
    return pl.pallas_call(
        functools.partial(_merge_ffn_kernel, final),
        grid=(M // tm, nf),
        in_specs=[row(D), row(256), row(256), row(512), row(256), row(256),
                  const((1, 256)), const((1, 256)), const((512, 256)), const((D, D)), const((1, D)),
                  pl.BlockSpec((D, tf), lambda t, j: (0, j)),
                  pl.BlockSpec((D, tf), lambda t, j: (0, j + nf)),
                  pl.BlockSpec((tf, D), lambda t, j: (j, 0)),
                  const((1, D))],
        out_specs=row(D),
        out_shape=jax.ShapeDtypeStruct((M, D), F32),
        scratch_shapes=[pltpu.VMEM((tm, D), F32), pltpu.VMEM((tm, D), BF16)],
        compiler_params=_cparams(("parallel", "arbitrary")),
        name="merge_ffn",
    )(x, ohg, ghg, ofox, ogla, ggla, pw["hgn"][layer], pw["glan"][layer], pw["gsum"],
      pw["w_out"][layer], pw["ffn_norm"][layer], pw["w_gu"][layer], pw["w_gu"][layer],
      pw["w_down"][layer], pw["final_norm"])


LIN_STEP_SEQS = 8


def _lin_step_kernel(s_ref, q_ref, k_ref, g_ref, v_ref, o_ref, sn_ref):
    nb, nh = s_ref.shape[:2]
    for b in range(nb):
        for h in range(nh):
            s_new = jnp.exp(g_ref[b, h]) * s_ref[b, h] + k_ref[b, h] * v_ref[b, h]
            sn_ref[b, h] = s_new
            o_ref[b, h] = jnp.sum(q_ref[b, h] * s_new, axis=0, keepdims=True)


def _lin_step(state, q, k, g, v, name):
    B, nh, dk, dv = state.shape
    col = lambda a: a.reshape(B, nh, dk, 1)
    nb = LIN_STEP_SEQS if B % LIN_STEP_SEQS == 0 else 1
    cspec = pl.BlockSpec((nb, nh, dk, 1), lambda b: (b, 0, 0, 0))
    sspec = pl.BlockSpec((nb, nh, dk, dv), lambda b: (b, 0, 0, 0))
    vspec = pl.BlockSpec((nb, nh, 1, dv), lambda b: (b, 0, 0, 0))
    o, sn = pl.pallas_call(
        _lin_step_kernel,
        grid=(B // nb,),
        in_specs=[sspec, cspec, cspec, cspec, vspec],
        out_specs=[vspec, sspec],
        out_shape=[jax.ShapeDtypeStruct((B, nh, 1, dv), F32),
                   jax.ShapeDtypeStruct((B, nh, dk, dv), F32)],
        compiler_params=_cparams(("parallel",)),
        name=name,
    )(state, col(q), col(k), col(g), v.reshape(B, nh, 1, dv))
    return o.reshape(B, nh * dv), sn


PAGES_PER_STEP = 16


def _fox_decode_kernel(npg, n_pages, pt_ref, *refs):
    qrow_ref, krow_ref, qcol_ref, vcol_ref, lfnew_ref, u_ref, lf_ref = refs[:7]
    k_refs = refs[7:7 + npg]
    v_refs = refs[7 + npg:7 + 2 * npg]
    o_ref = refs[7 + 2 * npg]
    qb_scr, m_scr, l_scr, t_scr, acc_scr, s_scr, p_scr = refs[8 + 2 * npg:]
    j = pl.program_id(1)
    pt0 = pl.program_id(0) * n_pages + (pl.num_programs(1) - 1 - j) * npg
    nh, hd, page = qb_scr.shape

    @pl.when(j == 0)
    def _():
        qb_scr[...] = jnp.broadcast_to(qcol_ref[0], (nh, hd, page))
        m_scr[...] = jnp.sum(qrow_ref[0] * krow_ref[0], axis=1, keepdims=True)
        l_scr[...] = jnp.ones_like(l_scr)
        t_scr[...] = lfnew_ref[0]
        acc_scr[...] = jnp.where(_iota((nh, hd, page), 2) == 0,
                                 jnp.broadcast_to(vcol_ref[0], (nh, hd, page)), 0.0)

    u = u_ref[...]
    t_run = t_scr[...]
    for i in reversed(range(npg)):
        x = lf_ref[pt_ref[pt0 + i]]
        x1, x2, x3 = _split3(x)
        cat = jnp.concatenate([x1.astype(F32), x2.astype(F32), x3.astype(F32),
                               jnp.zeros_like(x)], axis=0).astype(BF16)
        y4 = _dot(cat, u)
        y = y4[0:nh] + y4[nh:2 * nh] + y4[2 * nh:3 * nh]
        for h in range(nh):
            s_scr[i, h:h + 1, :] = jnp.sum(k_refs[i][h] * qb_scr[h], axis=0, keepdims=True)
        s_scr[i] = s_scr[i] + (y + t_run)
        t_run = t_run + (y[:, 0:1] + x[:, 0:1])
    t_scr[...] = t_run
    s_all = s_scr[...]
    m_prev = m_scr[...]
    m_new = jnp.maximum(m_prev, jnp.max(jnp.max(s_all, axis=0), axis=1, keepdims=True))
    alpha = jnp.exp(m_prev - m_new)
    p_all = jnp.exp(s_all - m_new)
    p_scr[...] = p_all
    l_scr[...] = alpha * l_scr[...] + jnp.sum(jnp.sum(p_all, axis=0), axis=1, keepdims=True)
    m_scr[...] = m_new
    for h in range(nh):
        contrib = jnp.zeros((hd, page), F32)
        for i in range(npg):
            contrib = contrib + p_scr[i, h:h + 1, :] * v_refs[i][h]
        acc_scr[h] = acc_scr[h] * alpha[h:h + 1, :] + contrib

    @pl.when(j == pl.num_programs(1) - 1)
    def _():
        l = l_scr[...]
        for h in range(nh):
            o_ref[0, h] = jnp.sum(acc_scr[h], axis=1, keepdims=True) / l[h:h + 1, :]


def _fox_decode(q, knew, vnew, lfnew, cache_k, cache_v, cache_lf, page_table, layer):
    B, W = q.shape
    depth, n_pool, page, nh, hd = cache_k.shape
    n_pages = page_table.shape[1]
    npg = PAGES_PER_STEP
    assert n_pages % npg == 0
    ng = n_pages // npg
    ck = jnp.transpose(cache_k, (0, 1, 3, 4, 2))
    cv = jnp.transpose(cache_v, (0, 1, 3, 4, 2))
    clf = jnp.transpose(cache_lf, (0, 1, 3, 2))
    u = jnp.asarray(np.tril(np.ones((page, page), np.float32), -1), BF16)
    pt_flat = page_table.reshape(-1)

    def page_map(nd):
        def make(i):
            def index_map(b, j, pt):
                return (layer, pt[b * n_pages + (ng - 1 - j) * npg + i]) + (0,) * nd
            return index_map
        return make

    per_b = lambda shp: pl.BlockSpec((1,) + shp, lambda b, j, pt: (b,) + (0,) * len(shp))
    in_specs = [per_b((nh, hd)), per_b((nh, hd)), per_b((nh, hd, 1)), per_b((nh, hd, 1)),
                per_b((nh, 1)), pl.BlockSpec((page, page), lambda b, j, pt: (0, 0)),
                pl.BlockSpec((None, n_pool, nh, page), lambda b, j, pt: (layer, 0, 0, 0),
                             pipeline_mode=pl.Buffered(1))]
    in_specs += [pl.BlockSpec((None, None, nh, hd, page), page_map(3)(i)) for i in range(npg)]
    in_specs += [pl.BlockSpec((None, None, nh, hd, page), page_map(3)(i)) for i in range(npg)]
    out = pl.pallas_call(
        functools.partial(_fox_decode_kernel, npg, n_pages),
        grid_spec=pltpu.PrefetchScalarGridSpec(
            num_scalar_prefetch=1,
            grid=(B, ng),
            in_specs=in_specs,
            out_specs=pl.BlockSpec((1, nh, hd, 1), lambda b, j, pt: (b, 0, 0, 0)),
            scratch_shapes=[pltpu.VMEM((nh, hd, page), F32), pltpu.VMEM((nh, 1), F32),
                            pltpu.VMEM((nh, 1), F32), pltpu.VMEM((nh, 1), F32),
                            pltpu.VMEM((nh, hd, page), F32),
                            pltpu.VMEM((npg, nh, page), F32), pltpu.VMEM((npg, nh, page), F32)]),
        out_shape=jax.ShapeDtypeStruct((B, nh, hd, 1), F32),
        compiler_params=_cparams(("parallel", "arbitrary")),
        name="fox_decode",
    )(pt_flat, q.reshape(B, nh, hd), knew.reshape(B, nh, hd), q.reshape(B, nh, hd, 1),
      vnew.reshape(B, nh, hd, 1), lfnew.reshape(B, nh, 1), u,
      clf, *([ck] * npg), *([cv] * npg))
    return out.reshape(B, W)


def _prepare(attn_norm, w_in, fox_f_bias, gla_gate_w2, gla_gate_b, hgrn_lb_logits, hgrn_out_norm,
             gla_out_norm, w_out, ffn_norm, w_gate_up, w_down, final_norm):
    depth, D, in_width = w_in.shape
    n_fh = fox_f_bias.shape[1]
    main = 3328
    off_ff = 2560
    off_glr = in_width - GLA_GATE_RANK
    assert off_glr == off_ff + n_fh + (main - off_ff)
    w_main = jnp.concatenate([w_in[:, :, :off_ff], w_in[:, :, off_ff + n_fh:off_glr]], axis=2)
    misc = jnp.concatenate([w_in[:, :, off_ff:off_ff + n_fh], w_in[:, :, off_glr:],
                            jnp.zeros((depth, D, LANES - n_fh - GLA_GATE_RANK), w_in.dtype)], axis=2)
    w_p = jnp.concatenate([w_main, misc], axis=2).astype(BF16)
    assert w_p.shape[2] == W_PAD
    w_kvt = jnp.transpose(w_in[:, :, 1536:1536 + 2 * FOX_W], (0, 2, 1)).astype(BF16)
    fbias = jnp.zeros((depth, 1, LANES), F32).at[:, 0, :n_fh].set(fox_f_bias)
    w2pad = jnp.zeros((depth, LANES, LANES), F32).at[:, n_fh:n_fh + GLA_GATE_RANK, :].set(gla_gate_w2)
    ff = w_down.shape[1]
    tf = ff // 2 if (ff // 2) % LANES == 0 else ff
    n_hg_heads = 256 // HG_DV
    gsum = np.kron(np.eye(n_hg_heads, dtype=np.float32), np.full((HG_DV, HG_DV), 1.0 / HG_DV, np.float32))
    return dict(
        n_fox_heads=n_fh,
        attn_norm=attn_norm.reshape(depth, 1, D),
        w_in=w_p,
        w_kvt=w_kvt,
        fbias=fbias,
        lb_logits=hgrn_lb_logits,
        w2pad=w2pad.astype(BF16),
        gla_b=gla_gate_b.reshape(depth, 1, LANES),
        hgn=jnp.tile(hgrn_out_norm, (1, n_hg_heads)).reshape(depth, 1, 256),
        glan=jnp.tile(gla_out_norm, (1, 256 // GLA_DV)).reshape(depth, 1, 256),
        gsum=jnp.asarray(np.concatenate([gsum, gsum], axis=0), BF16),
        w_out=w_out.astype(BF16),
        ffn_norm=ffn_norm.reshape(depth, 1, D),
        w_gu=w_gate_up.astype(BF16),
        w_down=w_down.astype(BF16),
        ff_tile=tf,
        final_norm=final_norm.reshape(1, D),
    )


def kernel(x_prompt, x_sample, cache_fox_k, cache_fox_v, cache_fox_logf, state_hgrn, state_gla, page_table, attn_norm, w_in, fox_f_bias, gla_gate_w2, gla_gate_b, hgrn_lb_logits, hgrn_out_norm, gla_out_norm, w_out, ffn_norm, w_gate_up, w_down, final_norm):
    B, L, D = x_prompt.shape
    DB = x_sample.shape[0]
    depth = w_in.shape[0]
    n_fh = fox_f_bias.shape[1]
    hd = FOX_HEAD_DIM
    pw = _prepare(attn_norm, w_in, fox_f_bias, gla_gate_w2, gla_gate_b, hgrn_lb_logits,
                  hgrn_out_norm, gla_out_norm, w_out, ffn_norm, w_gate_up, w_down, final_norm)
    tm = 512
    M = B * L
    xp = x_prompt
    xs = x_sample.reshape(1, DB, D)
    prev_kv = None
    lfp, hsp, gsp = [], [], []
    ksm, vsm, lfs, hss, gss = [], [], [], [], []
    r2 = lambda a: a.reshape(M, a.shape[-1])
    s2 = lambda a: a.reshape(DB, a.shape[-1])
    for l in range(depth):
        final = l == depth - 1
        a = _project(xp, l, True, tm, pw, prev_kv)
        prev_kv = (a["fkt"], a["fvt"])
        o_hg, s_hg = _gla_prompt(a["hq"], a["hk"], a["hlf"], a["hv"], HG_DK, HG_DV, "hgrn_prompt")
        o_fox = _fox_prompt(a["fq"], a["fk_b"], a["fvt_b"], a["c"], tm)
        o_gla, s_gla = _gla_prompt(a["gq"], a["gk"], a["glg"], a["gv"], GLA_DK, GLA_DV, "gla_prompt")
        xp = _merge_ffn(r2(xp), r2(o_hg), r2(a["hgate"]), r2(o_fox), r2(o_gla), r2(a["ggate"]),
                        l, final, tm, pw).reshape(B, L, D)
        lfp.append(a["flf"])
        hsp.append(s_hg)
        gsp.append(s_gla)
        a = _project(xs, l, False, DB, pw)
        o_hg, s_hg = _lin_step(state_hgrn[l], s2(a["hq"]), s2(a["hk"]), s2(a["hlf"]), s2(a["hv"]),
                               "hgrn_step")
        o_fox = _fox_decode(s2(a["fq"]).astype(F32), s2(a["fk"]), s2(a["fv"]), s2(a["flf"]),
                            cache_fox_k, cache_fox_v, cache_fox_logf, page_table, l)
        o_gla, s_gla = _lin_step(state_gla[l], s2(a["gq"]), s2(a["gk"]), s2(a["glg"]), s2(a["gv"]),
                                 "gla_step")
        xs = _merge_ffn(s2(xs), o_hg, s2(a["hgate"]), o_fox, o_gla, s2(a["ggate"]),
                        l, final, DB, pw).reshape(1, DB, D)
        ksm.append(a["fk"].reshape(DB, 1, n_fh, hd))
        vsm.append(a["fv"].reshape(DB, 1, n_fh, hd))
        lfs.append(a["flf"].reshape(DB, 1, n_fh))
        hss.append(s_hg)
        gss.append(s_gla)
    kp_all, vp_all = (jnp.transpose(t.reshape(depth, B, n_fh, hd, L), (0, 1, 4, 2, 3)) for t in prev_kv)
    return (xp, xs.reshape(DB, 1, D),
            kp_all, vp_all, jnp.stack(lfp), jnp.stack(hsp), jnp.stack(gsp),
            jnp.stack(ksm), jnp.stack(vsm), jnp.stack(lfs), jnp.stack(hss), jnp.stack(gss))
```

```python
import functools

import numpy as np
import jax
import jax.numpy as jnp
from jax import lax
from jax.experimental import pallas as pl
from jax.experimental.pallas import tpu as pltpu

F32 = jnp.float32
BF16 = jnp.bfloat16

HG_DK = 64
HG_DV = 64
FOX_HEAD_DIM = 64
GLA_DK = 32
GLA_DV = 64
GLA_GATE_RANK = 16
GLA_GATE_NORMALIZER = 16.0
RMS_EPS = 1e-6
LOG2E = 1.4426950408889634

LANES = 128
CHUNK = 64
SUB = 32
NSUB = CHUNK // SUB
GLA_SEQS_PER_STEP = 8
NEG_BIG = -1e30
VMEM_LIMIT = 48 * 1024 * 1024


def _cparams(sem):
    return pltpu.CompilerParams(dimension_semantics=sem, vmem_limit_bytes=VMEM_LIMIT)


def _shift(n):
    s = int(np.log2(n))
    assert (1 << s) == n
    return s


def _iota(shape, dim):
    return lax.broadcasted_iota(jnp.int32, shape, dim)


def _split3(x):
    x1 = x.astype(BF16)
    r1 = x - x1.astype(F32)
    x2 = r1.astype(BF16)
    r2 = r1 - x2.astype(F32)
    return x1, x2, r2.astype(BF16)


def _log_sigmoid(x):
    return jnp.minimum(x, 0.0) - jnp.log1p(jnp.exp(-jnp.abs(x)))


def _sigmoid(x):
    return 1.0 / (1.0 + jnp.exp(-x))


def _dot(a, b):
    return jnp.dot(a, b, preferred_element_type=F32)


def _dot_nt(a, b):
    return lax.dot_general(a, b, (((1,), (1,)), ((), ())), preferred_element_type=F32)


def _dot_tn(a, b):
    return lax.dot_general(a, b, (((0,), (0,)), ((), ())), preferred_element_type=F32)


def _rms(x, w):
    ms = jnp.mean(x * x, axis=-1, keepdims=True)
    return x * lax.rsqrt(ms + RMS_EPS) * w


W_PAD = 3456
MISC = 3328
FOX_W = 512
FOX_VROWS = FOX_HEAD_DIM + 16
COMMON_OUT = ("hq", "hk", "hlf", "hv", "hgate", "fq", "flf", "gq", "gk", "gv", "glg", "ggate")
PROMPT_OUT = COMMON_OUT + ("fk_b", "fkt", "fvt", "fvt_b", "c")
SAMPLE_OUT = COMMON_OUT + ("fk", "fv")


def _proj_kernel(layer, prompt, *refs):
    (x_ref, norm_ref, w_ref, fbias_ref, lbl_ref, w2_ref, gb_ref) = refs[:7]
    k = 7
    if prompt:
        wkvt_ref, tri_ref = refs[7], refs[8]
        k = 9
        if layer > 0:
            prev_kt_ref, prev_vt_ref = refs[9], refs[10]
            k = 11
    names = PROMPT_OUT if prompt else SAMPLE_OUT
    o = dict(zip(names, refs[k:k + len(names)]))
    if prompt:
        carry_ref = refs[k + len(names)]

    hb = _rms(x_ref[0], norm_ref[...]).astype(BF16)

    def seg(a, b):
        return _dot(hb, w_ref[:, a:b])

    lbl = lbl_ref[...]
    e = jnp.exp(lbl - jnp.max(lbl, axis=0, keepdims=True))
    sm = e / jnp.sum(e, axis=0, keepdims=True)
    lb = jnp.zeros((1, lbl.shape[1]), F32)
    for i in range(1, layer + 1):
        lb = lb + sm[i:i + 1, :]
    o["hq"][0] = seg(0, 256)
    f = lb + (1.0 - lb) * _sigmoid(seg(256, 512))
    o["hlf"][0] = jnp.log(f)
    o["hk"][0] = 1.0 - f
    o["hv"][0] = seg(512, 768).astype(o["hv"].dtype)
    o["hgate"][0] = seg(768, 1024)
    q_scale = FOX_HEAD_DIM ** -0.5 * (LOG2E if prompt else 1.0)
    o["fq"][0] = (seg(1024, 1536) * q_scale).astype(BF16)
    if prompt:
        kvt = _dot_nt(wkvt_ref[...], hb)
        o["fk_b"][0] = kvt[0:FOX_W].T.astype(BF16)
        o["fkt"][layer, 0] = kvt[0:FOX_W]
        o["fvt"][layer, 0] = kvt[FOX_W:2 * FOX_W]
        for lp in range(layer):
            o["fkt"][lp, 0] = prev_kt_ref[lp, 0]
            o["fvt"][lp, 0] = prev_vt_ref[lp, 0]
        hd = FOX_HEAD_DIM
        ones = jnp.ones((FOX_VROWS - hd, kvt.shape[1]), BF16)
        for h in range(FOX_W // hd):
            r = h * FOX_VROWS
            o["fvt_b"][0, r:r + hd, :] = kvt[FOX_W + h * hd:FOX_W + (h + 1) * hd].astype(BF16)
            o["fvt_b"][0, r + hd:r + FOX_VROWS, :] = ones
    else:
        o["fk"][0] = seg(1536, 2048)
        o["fv"][0] = seg(2048, 2560)
    o["gq"][0] = seg(2560, 2688) * (GLA_DK ** -0.5)
    o["gk"][0] = seg(2688, 2816)
    o["gv"][0] = seg(2816, 3072).astype(o["gv"].dtype)
    o["ggate"][0] = seg(3072, 3328)
    misc = seg(MISC, W_PAD)
    lf_all = _log_sigmoid(misc + fbias_ref[...])
    n_fh = o["flf"].shape[-1]
    o["flf"][0] = lf_all[:, 0:n_fh]
    g_logit = _dot(misc.astype(BF16), w2_ref[...]) + gb_ref[...]
    o["glg"][0] = _log_sigmoid(g_logit) / GLA_GATE_NORMALIZER

    if prompt:
        @pl.when(pl.program_id(1) == 0)
        def _():
            carry_ref[...] = jnp.zeros_like(carry_ref)

        x1, x2, x3 = _split3(lf_all)
        tri = tri_ref[...]
        c_all = _dot(tri, x1) + _dot(tri, x2) + _dot(tri, x3) + carry_ref[...]
        for g in range(n_fh // FOX_HG):
            o["c"][0, g] = c_all[:, g * FOX_HG:(g + 1) * FOX_HG] * LOG2E
        tm = c_all.shape[0]
        carry_ref[...] = c_all[tm - 1:tm, :]


def _project(x3, layer, prompt, tm, pw, prev_kv=None):
    B, L, D = x3.shape
    n_fh = pw["n_fox_heads"]
    grid = (B, L // tm)
    row = lambda w: pl.BlockSpec((1, tm, w), lambda b, t: (b, t, 0))
    const = lambda shp: pl.BlockSpec(shp, lambda b, t: (0,) * len(shp))
    v_dtype = BF16 if prompt else F32
    rows = dict(hq=(256, F32), hk=(256, F32), hlf=(256, F32), hv=(256, v_dtype), hgate=(256, F32),
                fq=(FOX_W, BF16), flf=(n_fh, F32), gq=(128, F32), gk=(128, F32), gv=(256, v_dtype),
                glg=(128, F32), ggate=(256, F32), fk_b=(FOX_W, BF16), c=(n_fh, F32),
                fk=(FOX_W, F32), fv=(FOX_W, F32))
    cols = dict(fkt=F32, fvt=F32, fvt_b=BF16)
    in_arrays = [x3, pw["attn_norm"][layer], pw["w_in"][layer], pw["fbias"][layer],
                 pw["lb_logits"], pw["w2pad"][layer], pw["gla_b"][layer]]
    in_specs = [row(D), const((1, D)), const((D, W_PAD)), const((1, LANES)),
                const(pw["lb_logits"].shape), const((LANES, LANES)), const((1, LANES))]
    scratch = []
    if prompt:
        tri = jnp.asarray(np.tril(np.ones((tm, tm), np.float32)), BF16)
        in_arrays += [pw["w_kvt"][layer], tri]
        in_specs += [const((2 * FOX_W, D)), const((tm, tm))]
        scratch.append(pltpu.VMEM((1, LANES), F32))
        if layer > 0:
            in_arrays += list(prev_kv)
            in_specs += [pl.BlockSpec((layer, 1, FOX_W, tm), lambda b, t: (0, b, 0, t))] * 2
    names = PROMPT_OUT if prompt else SAMPLE_OUT
    out_shape, out_specs = [], []
    for n in names:
        if n == "fvt_b":
            nrow = FOX_W // FOX_HEAD_DIM * FOX_VROWS
            out_shape.append(jax.ShapeDtypeStruct((B, nrow, L), cols[n]))
            out_specs.append(pl.BlockSpec((1, nrow, tm), lambda b, t: (b, 0, t)))
        elif n in cols:
            out_shape.append(jax.ShapeDtypeStruct((layer + 1, B, FOX_W, L), cols[n]))
            out_specs.append(pl.BlockSpec((layer + 1, 1, FOX_W, tm), lambda b, t: (0, b, 0, t)))
        elif n == "c":
            n_hg = n_fh // FOX_HG
            out_shape.append(jax.ShapeDtypeStruct((B, n_hg, L, FOX_HG), F32))
            out_specs.append(pl.BlockSpec((1, n_hg, tm, FOX_HG), lambda b, t: (b, 0, t, 0)))
        else:
            w, dt = rows[n]
            out_shape.append(jax.ShapeDtypeStruct((B, L, w), dt))
            out_specs.append(row(w))
    outs = pl.pallas_call(
        functools.partial(_proj_kernel, layer, prompt),
        grid=grid, in_specs=in_specs, out_specs=out_specs, out_shape=out_shape,
        scratch_shapes=scratch,
        compiler_params=_cparams(("parallel", "arbitrary")),
        name="proj_prompt" if prompt else "proj_sample",
    )(*in_arrays)
    return dict(zip(names, outs))


def _gla_chunk_kernel(dk, dv, n_chunks, q_ref, k_ref, g_ref, v_ref, tt_ref, o_ref, st_ref, s_scr):
    hdk = q_ref.shape[-1]
    hdv = v_ref.shape[-1]
    nh = hdk // dk
    C = CHUNK

    @pl.when(pl.program_id(1) == 0)
    def _():
        s_scr[...] = jnp.zeros_like(s_scr)

    row_k = _iota((C, hdk), 0)
    head_k = _iota((C, hdk), 1) >> _shift(dk)
    head_v = _iota((C, hdv), 1) >> _shift(dv)
    a_row = _iota((C, nh * C), 0)
    a_col = _iota((C, nh * C), 1) & (C - 1)
    causal = a_col <= a_row
    bd_mask = (_iota((hdv, hdk), 0) >> _shift(dv)) == (_iota((hdv, hdk), 1) >> _shift(dk))

    def stage_cumsum(bb, r0):
        g1, g2, g3 = _split3(g_ref[bb, pl.ds(r0, C), :])
        gcat = jnp.concatenate([g1, g2, g3], axis=0)
        return _dot(tt_ref[...], gcat)

    def stage_scores(bb, r0, bw):
        q = q_ref[bb, pl.ds(r0, C), :]
        k = k_ref[bb, pl.ds(r0, C), :]
        b = bw[0:C]
        w = bw[C:2 * C]
        qt = q * jnp.exp(w)
        qb = (q * jnp.exp(b)).astype(BF16)
        bend = b[C - 1:C, :]
        kdec = (k * jnp.exp(bend - b)).astype(BF16)
        lhs_parts = []
        kt_parts = []
        for i in range(NSUB):
            in_sub = (row_k >> _shift(SUB)) == i
            lhs_parts.append(jnp.where(in_sub, qt, 0.0).astype(BF16))
            if i == 0:
                arg = jnp.where(row_k < SUB, -b, NEG_BIG)
            else:
                arg = jnp.where(row_k < SUB * (i + 1), b[SUB * i - 1:SUB * i, :] - b, NEG_BIG)
            kt_parts.append(k * jnp.exp(arg))
        lhs = jnp.concatenate(lhs_parts, axis=1)
        rt_rows = []
        for h in range(nh):
            rt_rows.append(jnp.concatenate(
                [jnp.where(head_k == h, kt, 0.0).astype(BF16) for kt in kt_parts], axis=1))
        rt = jnp.concatenate(rt_rows, axis=0)
        a_all = _dot_nt(lhs, rt)
        return a_all, qb, kdec, bend

    def stage_output(bb, r0, a_all, qb, kdec, bend):
        v = v_ref[bb, pl.ds(r0, C), :].astype(BF16)
        a_m = jnp.where(causal, a_all, 0.0).astype(BF16)
        vbd = jnp.concatenate([jnp.where(head_v == h, v, 0).astype(BF16) for h in range(nh)], axis=0)
        s_t = s_scr[bb]
        o = _dot(a_m, vbd) + _dot_nt(qb, s_t.astype(BF16))
        upd = _dot_tn(v, kdec)
        return o, s_t * jnp.exp(bend) + jnp.where(bd_mask, upd, 0.0)

    def body(ci, carry):
        r0 = pl.multiple_of(ci * C, C)
        seqs = range(q_ref.shape[0])
        bws = [stage_cumsum(bb, r0) for bb in seqs]
        mids = [stage_scores(bb, r0, bws[bb]) for bb in seqs]
        res = [stage_output(bb, r0, *mids[bb]) for bb in seqs]
        o_ref[:, pl.ds(r0, C), :] = jnp.stack([r[0] for r in res], axis=0)
        s_scr[...] = jnp.stack([r[1] for r in res], axis=0)
        return carry

    lax.fori_loop(0, n_chunks, body, 0)

    @pl.when(pl.program_id(1) == pl.num_programs(1) - 1)
    def _():
        st_ref[...] = s_scr[...]


def _gla_prompt(q, k, g, v, dk, dv, name):
    B, L, hdk = q.shape
    hdv = v.shape[-1]
    nh = hdk // dk
    assert nh == hdv // dv and nh * CHUNK <= 256
    rows = 256 if L % 256 == 0 else L
    assert rows % CHUNK == 0 and L % rows == 0
    C = CHUNK
    t_full = np.tril(np.ones((C, C), np.float32))
    same_sub = (np.arange(C)[:, None] // SUB) == (np.arange(C)[None, :] // SUB)
    t_sub = t_full * same_sub
    tt = np.concatenate([np.tile(t_full, (1, 3)), np.tile(t_sub, (1, 3))], axis=0)
    tt = jnp.asarray(tt, BF16)
    nb = GLA_SEQS_PER_STEP if B % GLA_SEQS_PER_STEP == 0 else 1
    blk = lambda w: pl.BlockSpec((nb, rows, w), lambda b, t: (b, t, 0))
    o, st = pl.pallas_call(
        functools.partial(_gla_chunk_kernel, dk, dv, rows // C),
        grid=(B // nb, L // rows),
        in_specs=[blk(hdk), blk(hdk), blk(hdk), blk(hdv),
                  pl.BlockSpec((2 * C, 3 * C), lambda b, t: (0, 0))],
        out_specs=[blk(hdv), pl.BlockSpec((nb, hdv, hdk), lambda b, t: (b, 0, 0))],
        out_shape=[jax.ShapeDtypeStruct((B, L, hdv), F32),
                   jax.ShapeDtypeStruct((B, hdv, hdk), F32)],
        scratch_shapes=[pltpu.VMEM((nb, hdv, hdk), F32)],
        compiler_params=_cparams(("parallel", "arbitrary")),
        name=name,
    )(q, k, g, v, tt)
    st5 = st.reshape(B, nh, dv, nh, dk)
    idx = jnp.arange(nh)
    state = st5[:, idx, :, idx, :]
    return o, jnp.transpose(state, (1, 0, 3, 2))


FOX_HG = 4
FOX_TILE = 1024


def _fox_prompt_kernel(qi_ref, kj_ref, q_ref, k_ref, vt_ref, c_ref, o_ref, qm_scr, m_scr, acc_scr):
    i = qi_ref[pl.program_id(2)]
    j = kj_ref[pl.program_id(2)]
    tq = q_ref.shape[1]
    tk = k_ref.shape[1]
    w = q_ref.shape[2]
    hd = FOX_HEAD_DIM
    vr = FOX_VROWS

    @pl.when(j == 0)
    def _():
        q = q_ref[0]
        lane_head = _iota((tq, w), 1) >> _shift(hd)
        for h in range(FOX_HG):
            qm_scr[h] = jnp.where(lane_head == h, q, 0).astype(BF16)
        m_scr[...] = jnp.full_like(m_scr, -jnp.inf)
        acc_scr[...] = jnp.zeros_like(acc_scr)

    def compute(masked):
        k4 = k_ref[0]
        c = c_ref[0, 0]
        if masked:
            keep = _iota((tk, tq), 0) <= _iota((tk, tq), 1)
        def scores(h):
            s = _dot_nt(k4, qm_scr[h]) - c[:, h:h + 1]
            if masked:
                s = jnp.where(keep, s, -jnp.inf)
            return s, jnp.max(s, axis=0, keepdims=True)

        def softmax_pv(h, s, mx):
            m_prev = m_scr[h]
            m_new = jnp.maximum(m_prev, mx)
            alpha = jnp.exp2(m_prev - m_new)
            p = jnp.exp2(s - m_new)
            m_scr[h] = m_new
            pv = _dot(vt_ref[0, h * vr:(h + 1) * vr, :], p.astype(BF16))
            acc_scr[h] = acc_scr[h] * alpha + pv

        pending = [scores(h) for h in range(FOX_HG)]
        for h in range(FOX_HG):
            softmax_pv(h, *pending[h])

    @pl.when(j < i)
    def _():
        compute(False)

    @pl.when(j == i)
    def _():
        compute(True)
        ot = jnp.concatenate([acc_scr[h, 0:hd, :] / acc_scr[h, hd:hd + 1, :] for h in range(FOX_HG)],
                             axis=0)
        o_ref[0] = ot.T


def _fox_prompt(q, k, vt, c, tile):
    B, L, W = q.shape
    hd = FOX_HEAD_DIM
    gw = FOX_HG * hd
    n_hg = W // gw
    nq = L // tile
    pairs = [(i, j) for i in range(nq) for j in range(i + 1)]
    qi = jnp.asarray([p[0] for p in pairs], jnp.int32)
    kj = jnp.asarray([p[1] for p in pairs], jnp.int32)
    qspec = pl.BlockSpec((1, tile, gw), lambda b, g, s, qi, kj: (b, qi[s], g))
    kspec = pl.BlockSpec((1, tile, gw), lambda b, g, s, qi, kj: (b, kj[s], g))
    vspec = pl.BlockSpec((1, FOX_HG * FOX_VROWS, tile), lambda b, g, s, qi, kj: (b, g, kj[s]))
    c4 = c
    cspec = pl.BlockSpec((1, 1, tile, FOX_HG), lambda b, g, s, qi, kj: (b, g, kj[s], 0))
    return pl.pallas_call(
        _fox_prompt_kernel,
        grid_spec=pltpu.PrefetchScalarGridSpec(
            num_scalar_prefetch=2,
            grid=(B, n_hg, len(pairs)),
            in_specs=[qspec, kspec, vspec, cspec],
            out_specs=qspec,
            scratch_shapes=[pltpu.VMEM((FOX_HG, tile, gw), BF16),
                            pltpu.VMEM((FOX_HG, 1, tile), F32),
                            pltpu.VMEM((FOX_HG, FOX_VROWS, tile), F32)]),
        out_shape=jax.ShapeDtypeStruct((B, L, W), F32),
        compiler_params=_cparams(("parallel", "parallel", "arbitrary")),
        name="fox_prompt",
    )(qi, kj, q, k, vt, c4)


def _head_norm_gate(o, gate, w_row, gsum):
    y = o * o
    y1 = y.astype(BF16)
    y2 = (y - y1.astype(F32)).astype(BF16)
    ms = _dot(jnp.concatenate([y1, y2], axis=1), gsum)
    return o * lax.rsqrt(ms + RMS_EPS) * w_row * (gate * _sigmoid(gate))


def _merge_ffn_kernel(final, x_ref, ohg_ref, ghg_ref, ofox_ref, ogla_ref, ggla_ref, hgn_ref, glan_ref,
                      gsum_ref, wout_ref, ffn_norm_ref, wg_ref, wu_ref, wd_ref, fn_ref, o_ref,
                      acc_ref, h2_ref):
    j = pl.program_id(1)

    @pl.when(j == 0)
    def _():
        gsum = gsum_ref[...]
        hg = _head_norm_gate(ohg_ref[...], ghg_ref[...], hgn_ref[...], gsum)
        gla = _head_norm_gate(ogla_ref[...], ggla_ref[...], glan_ref[...], gsum)
        mix = jnp.concatenate([hg.astype(BF16), ofox_ref[...].astype(BF16), gla.astype(BF16)], axis=1)
        x1 = x_ref[...] + _dot(mix, wout_ref[...])
        acc_ref[...] = x1
        h2_ref[...] = _rms(x1, ffn_norm_ref[...]).astype(BF16)

    h = h2_ref[...]
    gate = _dot(h, wg_ref[...])
    up = _dot(h, wu_ref[...])
    act = (gate * _sigmoid(gate) * up).astype(BF16)
    acc_ref[...] += _dot(act, wd_ref[...])

    @pl.when(j == pl.num_programs(1) - 1)
    def _():
        x2 = acc_ref[...]
        o_ref[...] = _rms(x2, fn_ref[...]) if final else x2


def _merge_ffn(x, ohg, ghg, ofox, ogla, ggla, layer, final, tm, pw):
    M, D = x.shape
    ff = pw["w_down"][layer].shape[0]
    tf = pw["ff_tile"]
    nf = ff // tf
    row = lambda w: pl.BlockSpec((tm, w), lambda t, j: (t, 0))
    const = lambda shp: pl.BlockSpec(shp, lambda t, j: (0,) * len(shp))
    return pl.pallas_call(
        functools.partial(_merge_ffn_kernel, final),
        grid=(M // tm, nf),
        in_specs=[row(D), row(256), row(256), row(512), row(256), row(256),
                  const((1, 256)), const((1, 256)), const((512, 256)), const((D, D)), const((1, D)),
                  pl.BlockSpec((D, tf), lambda t, j: (0, j)),
                  pl.BlockSpec((D, tf), lambda t, j: (0, j + nf)),
                  pl.BlockSpec((tf, D), lambda t, j: (j, 0)),
                  const((1, D))],
        out_specs=row(D),
        out_shape=jax.ShapeDtypeStruct((M, D), F32),
        scratch_shapes=[pltpu.VMEM((tm, D), F32), pltpu.VMEM((tm, D), BF16)],
        compiler_params=_cparams(("parallel", "arbitrary")),
        name="merge_ffn",
    )(x, ohg, ghg, ofox, ogla, ggla, pw["hgn"][layer], pw["glan"][layer], pw["gsum"],
      pw["w_out"][layer], pw["ffn_norm"][layer], pw["w_gu"][layer], pw["w_gu"][layer],
      pw["w_down"][layer], pw["final_norm"])


LIN_STEP_SEQS = 8


def _lin_step_kernel(s_ref, q_ref, k_ref, g_ref, v_ref, o_ref, sn_ref):
    nb, nh = s_ref.shape[:2]
    for b in range(nb):
        for h in range(nh):
            s_new = jnp.exp(g_ref[b, h]) * s_ref[b, h] + k_ref[b, h] * v_ref[b, h]
            sn_ref[b, h] = s_new
            o_ref[b, h] = jnp.sum(q_ref[b, h] * s_new, axis=0, keepdims=True)


def _lin_step(state, q, k, g, v, name):
    B, nh, dk, dv = state.shape
    col = lambda a: a.reshape(B, nh, dk, 1)
    nb = LIN_STEP_SEQS if B % LIN_STEP_SEQS == 0 else 1
    cspec = pl.BlockSpec((nb, nh, dk, 1), lambda b: (b, 0, 0, 0))
    sspec = pl.BlockSpec((nb, nh, dk, dv), lambda b: (b, 0, 0, 0))
    vspec = pl.BlockSpec((nb, nh, 1, dv), lambda b: (b, 0, 0, 0))
    o, sn = pl.pallas_call(
        _lin_step_kernel,
        grid=(B // nb,),
        in_specs=[sspec, cspec, cspec, cspec, vspec],
        out_specs=[vspec, sspec],
        out_shape=[jax.ShapeDtypeStruct((B, nh, 1, dv), F32),
                   jax.ShapeDtypeStruct((B, nh, dk, dv), F32)],
        compiler_params=_cparams(("parallel",)),
        name=name,
    )(state, col(q), col(k), col(g), v.reshape(B, nh, 1, dv))
    return o.reshape(B, nh * dv), sn


PAGES_PER_STEP = 16


def _fox_decode_kernel(npg, n_pages, pt_ref, *refs):
    qrow_ref, krow_ref, qcol_ref, vcol_ref, lfnew_ref, u_ref, lf_ref = refs[:7]
    k_refs = refs[7:7 + npg]
    v_refs = refs[7 + npg:7 + 2 * npg]
    o_ref = refs[7 + 2 * npg]
    qb_scr, m_scr, l_scr, t_scr, acc_scr, s_scr, p_scr = refs[8 + 2 * npg:]
    j = pl.program_id(1)
    pt0 = pl.program_id(0) * n_pages + (pl.num_programs(1) - 1 - j) * npg
    nh, hd, page = qb_scr.shape

    @pl.when(j == 0)
    def _():
        qb_scr[...] = jnp.broadcast_to(qcol_ref[0], (nh, hd, page))
        m_scr[...] = jnp.sum(qrow_ref[0] * krow_ref[0], axis=1, keepdims=True)
        l_scr[...] = jnp.ones_like(l_scr)
        t_scr[...] = lfnew_ref[0]
        acc_scr[...] = jnp.where(_iota((nh, hd, page), 2) == 0,
                                 jnp.broadcast_to(vcol_ref[0], (nh, hd, page)), 0.0)

    u = u_ref[...]
    t_run = t_scr[...]
    for i in reversed(range(npg)):
        x = lf_ref[pt_ref[pt0 + i]]
        x1, x2, x3 = _split3(x)
        cat = jnp.concatenate([x1.astype(F32), x2.astype(F32), x3.astype(F32),
                               jnp.zeros_like(x)], axis=0).astype(BF16)
        y4 = _dot(cat, u)
        y = y4[0:nh] + y4[nh:2 * nh] + y4[2 * nh:3 * nh]
        for h in range(nh):
            s_scr[i, h:h + 1, :] = jnp.sum(k_refs[i][h] * qb_scr[h], axis=0, keepdims=True)
        s_scr[i] = s_scr[i] + (y + t_run)
        t_run = t_run + (y[:, 0:1] + x[:, 0:1])
    t_scr[...] = t_run
    s_all = s_scr[...]
    m_prev = m_scr[...]
    m_new = jnp.maximum(m_prev, jnp.max(jnp.max(s_all, axis=0), axis=1, keepdims=True))
    alpha = jnp.exp(m_prev - m_new)
    p_all = jnp.exp(s_all - m_new)
    p_scr[...] = p_all
    l_scr[...] = alpha * l_scr[...] + jnp.sum(jnp.sum(p_all, axis=0), axis=1, keepdims=True)
    m_scr[...] = m_new
    for h in range(nh):
        contrib = jnp.zeros((hd, page), F32)
        for i in range(npg):
            contrib = contrib + p_scr[i, h:h + 1, :] * v_refs[i][h]
        acc_scr[h] = acc_scr[h] * alpha[h:h + 1, :] + contrib

    @pl.when(j == pl.num_programs(1) - 1)
    def _():
        l = l_scr[...]
        for h in range(nh):
            o_ref[0, h] = jnp.sum(acc_scr[h], axis=1, keepdims=True) / l[h:h + 1, :]


def _fox_decode(q, knew, vnew, lfnew, cache_k, cache_v, cache_lf, page_table, layer):
    B, W = q.shape
    depth, n_pool, page, nh, hd = cache_k.shape
    n_pages = page_table.shape[1]
    npg = PAGES_PER_STEP
    assert n_pages % npg == 0
    ng = n_pages // npg
    ck = jnp.transpose(cache_k, (0, 1, 3, 4, 2))
    cv = jnp.transpose(cache_v, (0, 1, 3, 4, 2))
    clf = jnp.transpose(cache_lf, (0, 1, 3, 2))
    u = jnp.asarray(np.tril(np.ones((page, page), np.float32), -1), BF16)
    pt_flat = page_table.reshape(-1)

    def page_map(nd):
        def make(i):
            def index_map(b, j, pt):
                return (layer, pt[b * n_pages + (ng - 1 - j) * npg + i]) + (0,) * nd
            return index_map
        return make

    per_b = lambda shp: pl.BlockSpec((1,) + shp, lambda b, j, pt: (b,) + (0,) * len(shp))
    in_specs = [per_b((nh, hd)), per_b((nh, hd)), per_b((nh, hd, 1)), per_b((nh, hd, 1)),
                per_b((nh, 1)), pl.BlockSpec((page, page), lambda b, j, pt: (0, 0)),
                pl.BlockSpec((None, n_pool, nh, page), lambda b, j, pt: (layer, 0, 0, 0),
                             pipeline_mode=pl.Buffered(1))]
    in_specs += [pl.BlockSpec((None, None, nh, hd, page), page_map(3)(i)) for i in range(npg)]
    in_specs += [pl.BlockSpec((None, None, nh, hd, page), page_map(3)(i)) for i in range(npg)]
    out = pl.pallas_call(
        functools.partial(_fox_decode_kernel, npg, n_pages),
        grid_spec=pltpu.PrefetchScalarGridSpec(
            num_scalar_prefetch=1,
            grid=(B, ng),
            in_specs=in_specs,
            out_specs=pl.BlockSpec((1, nh, hd, 1), lambda b, j, pt: (b, 0, 0, 0)),
            scratch_shapes=[pltpu.VMEM((nh, hd, page), F32), pltpu.VMEM((nh, 1), F32),
                            pltpu.VMEM((nh, 1), F32), pltpu.VMEM((nh, 1), F32),
                            pltpu.VMEM((nh, hd, page), F32),
                            pltpu.VMEM((npg, nh, page), F32), pltpu.VMEM((npg, nh, page), F32)]),
        out_shape=jax.ShapeDtypeStruct((B, nh, hd, 1), F32),
        compiler_params=_cparams(("parallel", "arbitrary")),
        name="fox_decode",
    )(pt_flat, q.reshape(B, nh, hd), knew.reshape(B, nh, hd), q.reshape(B, nh, hd, 1),
      vnew.reshape(B, nh, hd, 1), lfnew.reshape(B, nh, 1), u,
      clf, *([ck] * npg), *([cv] * npg))
    return out.reshape(B, W)


def _prepare(attn_norm, w_in, fox_f_bias, gla_gate_w2, gla_gate_b, hgrn_lb_logits, hgrn_out_norm,
             gla_out_norm, w_out, ffn_norm, w_gate_up, w_down, final_norm):
    depth, D, in_width = w_in.shape
    n_fh = fox_f_bias.shape[1]
    main = 3328
    off_ff = 2560
    off_glr = in_width - GLA_GATE_RANK
    assert off_glr == off_ff + n_fh + (main - off_ff)
    w_main = jnp.concatenate([w_in[:, :, :off_ff], w_in[:, :, off_ff + n_fh:off_glr]], axis=2)
    misc = jnp.concatenate([w_in[:, :, off_ff:off_ff + n_fh], w_in[:, :, off_glr:],
                            jnp.zeros((depth, D, LANES - n_fh - GLA_GATE_RANK), w_in.dtype)], axis=2)
    w_p = jnp.concatenate([w_main, misc], axis=2).astype(BF16)
    assert w_p.shape[2] == W_PAD
    w_kvt = jnp.transpose(w_in[:, :, 1536:1536 + 2 * FOX_W], (0, 2, 1)).astype(BF16)
    fbias = jnp.zeros((depth, 1, LANES), F32).at[:, 0, :n_fh].set(fox_f_bias)
    w2pad = jnp.zeros((depth, LANES, LANES), F32).at[:, n_fh:n_fh + GLA_GATE_RANK, :].set(gla_gate_w2)
    ff = w_down.shape[1]
    tf = ff // 2 if (ff // 2) % LANES == 0 else ff
    n_hg_heads = 256 // HG_DV
    gsum = np.kron(np.eye(n_hg_heads, dtype=np.float32), np.full((HG_DV, HG_DV), 1.0 / HG_DV, np.float32))
    return dict(
        n_fox_heads=n_fh,
        attn_norm=attn_norm.reshape(depth, 1, D),
        w_in=w_p,
        w_kvt=w_kvt,
        fbias=fbias,
        lb_logits=hgrn_lb_logits,
        w2pad=w2pad.astype(BF16),
        gla_b=gla_gate_b.reshape(depth, 1, LANES),
        hgn=jnp.tile(hgrn_out_norm, (1, n_hg_heads)).reshape(depth, 1, 256),
        glan=jnp.tile(gla_out_norm, (1, 256 // GLA_DV)).reshape(depth, 1, 256),
        gsum=jnp.asarray(np.concatenate([gsum, gsum], axis=0), BF16),
        w_out=w_out.astype(BF16),
        ffn_norm=ffn_norm.reshape(depth, 1, D),
        w_gu=w_gate_up.astype(BF16),
        w_down=w_down.astype(BF16),
        ff_tile=tf,
        final_norm=final_norm.reshape(1, D),
    )


def kernel(x_prompt, x_sample, cache_fox_k, cache_fox_v, cache_fox_logf, state_hgrn, state_gla, page_table, attn_norm, w_in, fox_f_bias, gla_gate_w2, gla_gate_b, hgrn_lb_logits, hgrn_out_norm, gla_out_norm, w_out, ffn_norm, w_gate_up, w_down, final_norm):
    B, L, D = x_prompt.shape
    DB = x_sample.shape[0]
    depth = w_in.shape[0]
    n_fh = fox_f_bias.shape[1]
    hd = FOX_HEAD_DIM
    pw = _prepare(attn_norm, w_in, fox_f_bias, gla_gate_w2, gla_gate_b, hgrn_lb_logits,
                  hgrn_out_norm, gla_out_norm, w_out, ffn_norm, w_gate_up, w_down, final_norm)
    tm = 512
    M = B * L
    xp = x_prompt
    xs = x_sample.reshape(1, DB, D)
    prev_kv = None
    lfp, hsp, gsp = [], [], []
    ksm, vsm, lfs, hss, gss = [], [], [], [], []
    r2 = lambda a: a.reshape(M, a.shape[-1])
    s2 = lambda a: a.reshape(DB, a.shape[-1])
    for l in range(depth):
        final = l == depth - 1
        a = _project(xp, l, True, tm, pw, prev_kv)
        prev_kv = (a["fkt"], a["fvt"])
        o_hg, s_hg = _gla_prompt(a["hq"], a["hk"], a["hlf"], a["hv"], HG_DK, HG_DV, "hgrn_prompt")
        o_fox = _fox_prompt(a["fq"], a["fk_b"], a["fvt_b"], a["c"], FOX_TILE if L % FOX_TILE == 0 else tm)
        o_gla, s_gla = _gla_prompt(a["gq"], a["gk"], a["glg"], a["gv"], GLA_DK, GLA_DV, "gla_prompt")
        xp = _merge_ffn(r2(xp), r2(o_hg), r2(a["hgate"]), r2(o_fox), r2(o_gla), r2(a["ggate"]),
                        l, final, tm, pw).reshape(B, L, D)
        lfp.append(a["flf"])
        hsp.append(s_hg)
        gsp.append(s_gla)
        a = _project(xs, l, False, DB, pw)
        o_hg, s_hg = _lin_step(state_hgrn[l], s2(a["hq"]), s2(a["hk"]), s2(a["hlf"]), s2(a["hv"]),
                               "hgrn_step")
        o_fox = _fox_decode(s2(a["fq"]).astype(F32), s2(a["fk"]), s2(a["fv"]), s2(a["flf"]),
                            cache_fox_k, cache_fox_v, cache_fox_logf, page_table, l)
        o_gla, s_gla = _lin_step(state_gla[l], s2(a["gq"]), s2(a["gk"]), s2(a["glg"]), s2(a["gv"]),
                                 "gla_step")
        xs = _merge_ffn(s2(xs), o_hg, s2(a["hgate"]), o_fox, o_gla, s2(a["ggate"]),
                        l, final, DB, pw).reshape(1, DB, D)
        ksm.append(a["fk"].reshape(DB, 1, n_fh, hd))
        vsm.append(a["fv"].reshape(DB, 1, n_fh, hd))
        lfs.append(a["flf"].reshape(DB, 1, n_fh))
        hss.append(s_hg)
        gss.append(s_gla)
    kp_all, vp_all = (jnp.transpose(t.reshape(depth, B, n_fh, hd, L), (0, 1, 4, 2, 3)) for t in prev_kv)
    return (xp, xs.reshape(DB, 1, D),
            kp_all, vp_all, jnp.stack(lfp), jnp.stack(hsp), jnp.stack(gsp),
            jnp.stack(ksm), jnp.stack(vsm), jnp.stack(lfs), jnp.stack(hss), jnp.stack(gss))
```
